```python
import jax, jax.numpy as jnp
from jax import lax
import numpy as np

D_MODEL = 1024
BATCH = 8
SEQ = 4096
DEPTH = 4

GRID_W = 64
HEAD_DIM = 64
A_HEADS = 8
A_KV_HEADS = 2
B_HEADS = 8
B_KV_HEADS = 2
C_HEADS = D_MODEL // HEAD_DIM
Q_BLOCK = 128
WINDOW = 128
NA_KH = 8
NA_KW = 16
MEM_TOKENS = 256
MEM_HEADS = 4
MEM_HEAD_DIM = D_MODEL // MEM_HEADS
D_FF = 256 * ((8 * D_MODEL // 3 + 255) // 256)
ROPE_THETA = 10000.0
LN_EPS = 1e-5
RMS_EPS = 1e-6
ALPHA = (2.0 * DEPTH) ** 0.25
BETA = (8.0 * DEPTH) ** -0.25
N_EVEN = (DEPTH + 1) // 2
N_ODD = DEPTH // 2

A_Q = A_HEADS * HEAD_DIM
A_KV = A_KV_HEADS * HEAD_DIM
B_Q = B_HEADS * HEAD_DIM
B_KV = B_KV_HEADS * HEAD_DIM
AB_IN = A_Q + 2 * A_KV + B_Q + 2 * B_KV
AB_OUT = (A_HEADS + B_HEADS) * HEAD_DIM
AB_SPLITS = [A_Q, A_Q + A_KV, A_Q + 2 * A_KV, A_Q + 2 * A_KV + B_Q, A_Q + 2 * A_KV + B_Q + B_KV]
C_WIDTH = C_HEADS * HEAD_DIM

kernel_name = "hybrid_axial_window_neighbourhood_encoder"


def layer_norm(x, g, b):
    xf = x.astype(jnp.float32)
    mu = xf.mean(-1, keepdims=True)
    var = jnp.square(xf - mu).mean(-1, keepdims=True)
    return ((xf - mu) * lax.rsqrt(var + LN_EPS) * g.astype(jnp.float32) + b.astype(jnp.float32)).astype(x.dtype)


def rms_norm(x, g):
    xf = x.astype(jnp.float32)
    return (xf * lax.rsqrt(jnp.mean(xf * xf, -1, keepdims=True) + RMS_EPS) * g.astype(jnp.float32)).astype(x.dtype)


def swiglu(x, w_gate, w_up, w_down):
    return (jax.nn.silu(x @ w_gate) * (x @ w_up)) @ w_down


def rope_angles(pos, dim):
    inv = ROPE_THETA ** (-jnp.arange(0, dim, 2, dtype=jnp.float32) / dim)
    return pos[:, None] * inv[None, :]


def apply_rope(x, ang):
    half = x.shape[-1] // 2
    cos = jnp.cos(ang)[None, :, None, :].astype(x.dtype)
    sin = jnp.sin(ang)[None, :, None, :].astype(x.dtype)
    x1, x2 = x[..., :half], x[..., half:]
    return jnp.concatenate([x1 * cos - x2 * sin, x2 * cos + x1 * sin], axis=-1)


def global_gqa(q, k, v):
    B, S, H, d = q.shape
    Hkv = k.shape[2]
    G = H // Hkv
    nb = S // Q_BLOCK
    scale = d ** -0.5
    qb = q.reshape(B, nb, Q_BLOCK, Hkv, G, d).transpose(1, 0, 2, 3, 4, 5)

    def one_block(q_blk):
        s = jnp.einsum('bqkgd,bskd->bkgqs', q_blk, k).astype(jnp.float32) * scale
        p = jax.nn.softmax(s, axis=-1).astype(v.dtype)
        return jnp.einsum('bkgqs,bskd->bqkgd', p, v)

    o = lax.map(one_block, qb)
    return o.transpose(1, 0, 2, 3, 4, 5).reshape(B, S, H * d)


def window_gqa_sink(q, k, v, sink):
    B, S, H, d = q.shape
    Hkv = k.shape[2]
    G = H // Hkv
    nb = S // Q_BLOCK
    scale = d ** -0.5
    qb = q.reshape(B, nb, Q_BLOCK, Hkv, G, d)
    pad = ((0, 0), (Q_BLOCK, Q_BLOCK), (0, 0), (0, 0))
    kp = jnp.pad(k, pad).reshape(B, nb + 2, Q_BLOCK, Hkv, d)
    vp = jnp.pad(v, pad).reshape(B, nb + 2, Q_BLOCK, Hkv, d)
    k_band = jnp.concatenate([kp[:, :-2], kp[:, 1:-1], kp[:, 2:]], axis=2)
    v_band = jnp.concatenate([vp[:, :-2], vp[:, 1:-1], vp[:, 2:]], axis=2)
    blk = jnp.arange(nb)[:, None] * Q_BLOCK
    qi = blk + jnp.arange(Q_BLOCK)[None, :]
    kj = blk - Q_BLOCK + jnp.arange(3 * Q_BLOCK)[None, :]
    rel = kj[:, None, :] - qi[:, :, None]
    valid = (jnp.abs(rel) <= WINDOW) & (kj[:, None, :] >= 0) & (kj[:, None, :] < S)
    s = jnp.einsum('bnqkgd,bnskd->bnkgqs', qb, k_band).astype(jnp.float32) * scale
    s = jnp.where(valid[None, :, None, None], s, -jnp.inf)
    sink_l = sink.astype(jnp.float32).reshape(Hkv, G)[None, None, :, :, None, None]
    m = jnp.maximum(s.max(-1, keepdims=True), sink_l)
    p = jnp.exp(s - m)
    p = (p / (p.sum(-1, keepdims=True) + jnp.exp(sink_l - m))).astype(v.dtype)
    o = jnp.einsum('bnkgqs,bnskd->bnqkgd', p, v_band)
    return o.reshape(B, S, H * d)


def mixer_ab(h, w_in, w_out, q_gain, k_gain, sink, ang_2d, ang_1d):
    B, S, _ = h.shape
    qa, ka, va, qb, kb, vb = jnp.split(h @ w_in, AB_SPLITS, axis=-1)
    qa = qa.reshape(B, S, A_HEADS, HEAD_DIM)
    ka = ka.reshape(B, S, A_KV_HEADS, HEAD_DIM)
    va = va.reshape(B, S, A_KV_HEADS, HEAD_DIM)
    qa = apply_rope(rms_norm(qa, q_gain), ang_2d)
    ka = apply_rope(rms_norm(ka, k_gain), ang_2d)
    out_a = global_gqa(qa, ka, va)
    qb = apply_rope(qb.reshape(B, S, B_HEADS, HEAD_DIM), ang_1d)
    kb = apply_rope(kb.reshape(B, S, B_KV_HEADS, HEAD_DIM), ang_1d)
    vb = vb.reshape(B, S, B_KV_HEADS, HEAD_DIM)
    out_b = window_gqa_sink(qb, kb, vb, sink)
    return jnp.concatenate([out_a, out_b], axis=-1) @ w_out


def mixer_c(h, w_in, w_out, rpb):
    B, S, _ = h.shape
    rows = S // GRID_W
    kh = min(NA_KH, rows)
    kw = NA_KW
    scale = HEAD_DIM ** -0.5
    q, k, v = jnp.split(h @ w_in, 3, axis=-1)
    qg = q.reshape(B, rows, GRID_W, C_HEADS, HEAD_DIM)
    kg = k.reshape(B, rows, GRID_W, C_HEADS, HEAD_DIM)
    vg = v.reshape(B, rows, GRID_W, C_HEADS, HEAD_DIM)
    col = jnp.arange(GRID_W)
    col_start = jnp.clip(col - kw // 2, 0, GRID_W - kw)
    col_idx = col_start[:, None] + jnp.arange(kw)[None, :]
    dc = col_idx - col[:, None]

    def row_block(r):
        rs = jnp.clip(r - kh // 2, 0, rows - kh)
        k_rows = lax.dynamic_slice_in_dim(kg, rs, kh, axis=1)
        v_rows = lax.dynamic_slice_in_dim(vg, rs, kh, axis=1)
        k_nb = jnp.take(k_rows, col_idx, axis=2)
        v_nb = jnp.take(v_rows, col_idx, axis=2)
        q_r = lax.dynamic_index_in_dim(qg, r, axis=1, keepdims=False)
        dr = rs + jnp.arange(kh) - r
        bias = rpb[:, dr[None, :, None] + NA_KH - 1, dc[:, None, :] + NA_KW - 1]
        s = jnp.einsum('bwhd,bawjhd->bhwaj', q_r, k_nb).astype(jnp.float32) * scale
        s = s + bias.astype(jnp.float32)[None]
        p = jax.nn.softmax(s.reshape(B, C_HEADS, GRID_W, kh * kw), axis=-1)
        p = p.reshape(B, C_HEADS, GRID_W, kh, kw).astype(v.dtype)
        return jnp.einsum('bhwaj,bawjhd->bwhd', p, v_nb)

    o = lax.map(row_block, jnp.arange(rows))
    o = o.transpose(1, 0, 2, 3, 4).reshape(B, S, C_WIDTH)
    return o @ w_out


def memory_attn(h, mem, w_q, w_kv, w_o):
    B, S, _ = h.shape
    M = mem.shape[1]
    q = (h @ w_q).reshape(B, S, MEM_HEADS, MEM_HEAD_DIM)
    k, v = jnp.split(mem @ w_kv, 2, axis=-1)
    k = k.reshape(B, M, MEM_HEADS, MEM_HEAD_DIM)
    v = v.reshape(B, M, MEM_HEADS, MEM_HEAD_DIM)
    s = jnp.einsum('bshd,bmhd->bhsm', q, k).astype(jnp.float32) * (MEM_HEAD_DIM ** -0.5)
    p = jax.nn.softmax(s, axis=-1).astype(v.dtype)
    o = jnp.einsum('bhsm,bmhd->bshd', p, v).reshape(B, S, D_MODEL)
    return o @ w_o


def setup_inputs(seed: int = 0) -> dict:
    key = jax.random.key(seed)
    ks = jax.random.split(key, 20)
    nrm = jax.random.normal
    f32 = jnp.float32
    d_sc = D_MODEL ** -0.5
    return {
        "x": nrm(ks[0], (BATCH, SEQ, D_MODEL), f32),
        "mem": nrm(ks[1], (BATCH, MEM_TOKENS, D_MODEL), f32),
        "ln_g": 1.0 + 0.02 * nrm(ks[2], (DEPTH, 4, D_MODEL), f32),
        "ln_b": 0.02 * nrm(ks[3], (DEPTH, 4, D_MODEL), f32),
        "ffn_w_gate": nrm(ks[4], (DEPTH, 2, D_MODEL, D_FF), f32) * d_sc,
        "ffn_w_up": nrm(ks[5], (DEPTH, 2, D_MODEL, D_FF), f32) * d_sc,
        "ffn_w_down": nrm(ks[6], (DEPTH, 2, D_FF, D_MODEL), f32) * (D_FF ** -0.5) * BETA,
        "ab_w_in": nrm(ks[7], (N_EVEN, D_MODEL, AB_IN), f32) * d_sc,
        "ab_w_out": nrm(ks[8], (N_EVEN, AB_OUT, D_MODEL), f32) * (AB_OUT ** -0.5) * BETA,
        "ab_q_gain": 1.0 + 0.02 * nrm(ks[9], (N_EVEN, HEAD_DIM), f32),
        "ab_k_gain": 1.0 + 0.02 * nrm(ks[10], (N_EVEN, HEAD_DIM), f32),
        "ab_sink": 0.5 * nrm(ks[11], (N_EVEN, B_HEADS), f32),
        "c_w_in": nrm(ks[12], (N_ODD, D_MODEL, 3 * C_WIDTH), f32) * d_sc,
        "c_w_out": nrm(ks[13], (N_ODD, C_WIDTH, D_MODEL), f32) * (C_WIDTH ** -0.5) * BETA,
        "c_rpb": 0.1 * nrm(ks[14], (N_ODD, C_HEADS, 2 * NA_KH - 1, 2 * NA_KW - 1), f32),
        "mem_w_q": nrm(ks[15], (DEPTH, D_MODEL, D_MODEL), f32) * d_sc,
        "mem_w_kv": nrm(ks[16], (DEPTH, D_MODEL, 2 * D_MODEL), f32) * d_sc,
        "mem_w_o": nrm(ks[17], (DEPTH, D_MODEL, D_MODEL), f32) * d_sc * BETA,
    }


def reference(x, mem, ln_g, ln_b, ffn_w_gate, ffn_w_up, ffn_w_down,
              ab_w_in, ab_w_out, ab_q_gain, ab_k_gain, ab_sink,
              c_w_in, c_w_out, c_rpb, mem_w_q, mem_w_kv, mem_w_o):
    S = x.shape[1]
    t = jnp.arange(S)
    row = (t // GRID_W).astype(jnp.float32)
    colp = (t % GRID_W).astype(jnp.float32)
    ang_2d = jnp.concatenate([rope_angles(row, HEAD_DIM // 2), rope_angles(colp, HEAD_DIM // 2)], axis=-1)
    ang_1d = rope_angles(t.astype(jnp.float32), HEAD_DIM)
    for i in range(DEPTH):
        j = i // 2
        y = swiglu(x, ffn_w_gate[i, 0], ffn_w_up[i, 0], ffn_w_down[i, 0])
        x = layer_norm(ALPHA * x + 0.5 * y, ln_g[i, 0], ln_b[i, 0])
        if i % 2 == 0:
            y = mixer_ab(x, ab_w_in[j], ab_w_out[j], ab_q_gain[j], ab_k_gain[j], ab_sink[j], ang_2d, ang_1d)
        else:
            y = mixer_c(x, c_w_in[j], c_w_out[j], c_rpb[j])
        x = layer_norm(ALPHA * x + y, ln_g[i, 1], ln_b[i, 1])
        y = memory_attn(x, mem, mem_w_q[i], mem_w_kv[i], mem_w_o[i])
        x = layer_norm(ALPHA * x + y, ln_g[i, 2], ln_b[i, 2])
        y = swiglu(x, ffn_w_gate[i, 1], ffn_w_up[i, 1], ffn_w_down[i, 1])
        x = layer_norm(ALPHA * x + 0.5 * y, ln_g[i, 3], ln_b[i, 3])
    return x
```

```python
import functools

import numpy as np
import jax
import jax.numpy as jnp
from jax import lax
from jax.experimental import pallas as pl
from jax.experimental.pallas import tpu as pltpu

GRID_W = 64
HEAD_DIM = 64
A_HEADS = 8
A_KV_HEADS = 2
B_HEADS = 8
B_KV_HEADS = 2
Q_BLOCK = 128
WINDOW = 128
NA_KH = 8
NA_KW = 16
MEM_HEADS = 4
ROPE_THETA = 10000.0
LN_EPS = 1e-5
RMS_EPS = 1e-6

LANES = 128
VMEM_LIMIT_BYTES = 48 * 1024 * 1024

_BF16 = jnp.bfloat16
_F32 = jnp.float32
_NT = (((1,), (1,)), ((), ()))


def _dot(a, b):
    return jnp.dot(a, b, preferred_element_type=_F32)


def _dot_nt(a, b):
    return lax.dot_general(a, b, _NT, preferred_element_type=_F32)


def _params(*sem):
    return pltpu.CompilerParams(dimension_semantics=sem, vmem_limit_bytes=VMEM_LIMIT_BYTES)


def _layer_norm(z, g, b):
    mu = jnp.mean(z, axis=-1, keepdims=True)
    zc = z - mu
    var = jnp.mean(zc * zc, axis=-1, keepdims=True)
    return zc * lax.rsqrt(var + LN_EPS) * g + b


def _ffn_ln_kernel(x_ref, wg_ref, wu_ref, wd_ref, g_ref, b_ref, o_ref, acc_ref, xb_ref, *, alpha, nf):
    f = pl.program_id(1)

    @pl.when(f == 0)
    def _():
        xb_ref[...] = x_ref[...].astype(_BF16)
        acc_ref[...] = jnp.zeros_like(acc_ref)

    xb = xb_ref[...]
    gate = _dot(xb, wg_ref[...])
    up = _dot(xb, wu_ref[...])
    act = gate * jax.nn.sigmoid(gate) * up
    acc_ref[...] += _dot(act.astype(_BF16), wd_ref[...])

    @pl.when(f == nf - 1)
    def _():
        z = alpha * x_ref[...] + 0.5 * acc_ref[...]
        o_ref[...] = _layer_norm(z, g_ref[...], b_ref[...])


def _ffn_ln(x, wg, wu, wd, g, b, layer, half, *, alpha, tm, tf):
    M, D = x.shape
    F = wg.shape[-1]
    nf = F // tf
    return pl.pallas_call(
        functools.partial(_ffn_ln_kernel, alpha=alpha, nf=nf),
        grid=(M // tm, nf),
        in_specs=[
            pl.BlockSpec((tm, D), lambda i, f: (i, 0)),
            pl.BlockSpec((None, None, D, tf), lambda i, f: (layer, half, 0, f)),
            pl.BlockSpec((None, None, D, tf), lambda i, f: (layer, half, 0, f)),
            pl.BlockSpec((None, None, tf, D), lambda i, f: (layer, half, f, 0)),
            pl.BlockSpec((1, D), lambda i, f: (0, 0)),
            pl.BlockSpec((1, D), lambda i, f: (0, 0)),
        ],
        out_specs=pl.BlockSpec((tm, D), lambda i, f: (i, 0)),
        out_shape=jax.ShapeDtypeStruct((M, D), _F32),
        scratch_shapes=[pltpu.VMEM((tm, D), _F32), pltpu.VMEM((tm, D), _BF16)],
        compiler_params=_params("parallel", "arbitrary"),
        name="ffn_ln",
    )(x, wg, wu, wd, g, b)


def _proj_kernel(x_ref, w_ref, o_ref, *, scale):
    y = _dot(x_ref[...].astype(_BF16), w_ref[...])
    if scale != 1.0:
        y = y * scale
    o_ref[...] = y.astype(o_ref.dtype)


def _proj(x, w, *, scale=1.0, tm, tn):
    M, K = x.shape
    N = w.shape[1]
    return pl.pallas_call(
        functools.partial(_proj_kernel, scale=scale),
        grid=(M // tm, N // tn),
        in_specs=[pl.BlockSpec((tm, K), lambda i, j: (i, 0)), pl.BlockSpec((K, tn), lambda i, j: (0, j))],
        out_specs=pl.BlockSpec((tm, tn), lambda i, j: (i, j)),
        out_shape=jax.ShapeDtypeStruct((M, N), _BF16),
        compiler_params=_params("parallel", "parallel"),
        name="proj",
    )(x, w)


def _out_ln_kernel(*refs, alpha, n_in):
    a_refs = refs[:n_in]
    w_refs = refs[n_in:2 * n_in]
    x_ref, g_ref, b_ref, o_ref = refs[2 * n_in:]
    y = _dot(a_refs[0][...], w_refs[0][...])
    for a_ref, w_ref in zip(a_refs[1:], w_refs[1:]):
        y = y + _dot(a_ref[...], w_ref[...])
    o_ref[...] = _layer_norm(alpha * x_ref[...] + y, g_ref[...], b_ref[...])


def _out_ln(acts, ws, x, g, b, *, alpha, tm):
    M, D = x.shape
    n_in = len(acts)
    in_specs = [pl.BlockSpec((tm, a.shape[1]), lambda i: (i, 0)) for a in acts]
    in_specs += [pl.BlockSpec(w.shape, lambda i: (0, 0)) for w in ws]
    in_specs += [pl.BlockSpec((tm, D), lambda i: (i, 0)), pl.BlockSpec((1, D), lambda i: (0, 0)),
                 pl.BlockSpec((1, D), lambda i: (0, 0))]
    return pl.pallas_call(
        functools.partial(_out_ln_kernel, alpha=alpha, n_in=n_in),
        grid=(M // tm,),
        in_specs=in_specs,
        out_specs=pl.BlockSpec((tm, D), lambda i: (i, 0)),
        out_shape=jax.ShapeDtypeStruct((M, D), _F32),
        compiler_params=_params("parallel"),
        name="out_ln",
    )(*acts, *ws, x, g, b)


_AB_CHUNK = 256
_AB_LAYOUT = (("qa", 512), ("ka", 256), ("va", 256), ("qb", 512), ("kb", 256), ("vb", 256))


def _rope(y, cos, sin_lo, sin_hi):
    return y * cos + pltpu.roll(y, 96, axis=1) * sin_lo + pltpu.roll(y, 32, axis=1) * sin_hi


def _head_mean_square(y, bd):
    sq = y * y
    hi = sq.astype(_BF16)
    lo = (sq - hi.astype(_F32)).astype(_BF16)
    return _dot(hi, bd) + _dot(lo, bd)


def _ab_proj_kernel(x_ref, w_ref, bd_ref, qg_ref, kg_ref, c2_ref, sl2_ref, sh2_ref, c1_ref, sl1_ref, sh1_ref,
                    o_ref, *, scale):
    xb = x_ref[...].astype(_BF16)
    bd = bd_ref[...]
    col = 0
    for name, width in _AB_LAYOUT:
        for c in range(col, col + width, _AB_CHUNK):
            y2 = _dot(xb, w_ref[:, c:c + _AB_CHUNK])
            for h in range(0, _AB_CHUNK, LANES):
                y = y2[:, h:h + LANES]
                if name in ("qa", "ka"):
                    gain = qg_ref[...] if name == "qa" else kg_ref[...]
                    y = y * lax.rsqrt(_head_mean_square(y, bd) + RMS_EPS) * gain
                    y = _rope(y, c2_ref[...], sl2_ref[...], sh2_ref[...])
                elif name in ("qb", "kb"):
                    y = _rope(y, c1_ref[...], sl1_ref[...], sh1_ref[...])
                if name in ("qa", "qb"):
                    y = y * scale
                o_ref[:, c + h:c + h + LANES] = y.astype(o_ref.dtype)
        col += width


def _ab_proj(x, w_ext, bd, qg, kg, tabs2, tabs1, *, seq, tm):
    M, D = x.shape
    N = w_ext.shape[1]
    nt = seq // tm
    tab_spec = pl.BlockSpec((tm, LANES), lambda i: (i % nt, 0))
    const = lambda shape: pl.BlockSpec(shape, lambda i: (0, 0))
    return pl.pallas_call(
        functools.partial(_ab_proj_kernel, scale=HEAD_DIM ** -0.5),
        grid=(M // tm,),
        in_specs=[pl.BlockSpec((tm, D), lambda i: (i, 0)), const((D, N)), const((LANES, LANES)),
                  const((1, LANES)), const((1, LANES))] + [tab_spec] * 6,
        out_specs=pl.BlockSpec((tm, N), lambda i: (i, 0)),
        out_shape=jax.ShapeDtypeStruct((M, N), _BF16),
        compiler_params=_params("parallel"),
        name="ab_proj",
    )(x, w_ext, bd, qg, kg, *tabs2, *tabs1)


def _stack_heads(q):
    t = q.shape[0]
    lo = lax.broadcasted_iota(jnp.int32, (t, LANES), 1) < HEAD_DIM
    zero = jnp.zeros((t, LANES), q.dtype)
    parts = []
    for c in range(0, q.shape[1], LANES):
        pair = q[:, c:c + LANES]
        parts += [jnp.where(lo, pair, zero), jnp.where(lo, zero, pair)]
    return jnp.concatenate(parts, axis=0)


def _unstack_heads(o, t):
    lo = lax.broadcasted_iota(jnp.int32, (t, LANES), 1) < HEAD_DIM
    n = o.shape[0] // t
    return jnp.concatenate(
        [jnp.where(lo, o[(2 * i) * t:(2 * i + 1) * t], o[(2 * i + 1) * t:(2 * i + 2) * t]) for i in range(n // 2)],
        axis=1)


def _attn_a_kernel(q_ref, k_ref, v_ref, o_ref):
    tq = q_ref.shape[0]
    qs = _stack_heads(q_ref[...])
    s = _dot_nt(qs, k_ref[...])
    m = jnp.max(s, axis=-1, keepdims=True)
    p = jnp.exp(s - m)
    l = jnp.sum(p, axis=-1, keepdims=True)
    o = _dot(p.astype(_BF16), v_ref[...]) / l
    o_ref[...] = _unstack_heads(o, tq).astype(o_ref.dtype)


def _attn_a(qkv, *, batch, seq, tq):
    M = qkv.shape[0]
    nt = seq // tq
    gw = (A_HEADS // A_KV_HEADS) * HEAD_DIM
    k_col = A_HEADS * HEAD_DIM // LANES
    v_col = k_col + A_KV_HEADS
    return pl.pallas_call(
        _attn_a_kernel,
        grid=(batch, A_KV_HEADS, nt),
        in_specs=[
            pl.BlockSpec((tq, gw), lambda b, j, t: (b * nt + t, j)),
            pl.BlockSpec((seq, LANES), lambda b, j, t: (b, k_col + j)),
            pl.BlockSpec((seq, LANES), lambda b, j, t: (b, v_col + j)),
        ],
        out_specs=pl.BlockSpec((tq, gw), lambda b, j, t: (b * nt + t, j)),
        out_shape=jax.ShapeDtypeStruct((M, A_HEADS * HEAD_DIM), _BF16),
        compiler_params=_params("parallel", "parallel", "parallel"),
        name="attn_a",
    )(qkv, qkv, qkv)


def _attn_b_kernel(sink_ref, q_ref, k_ref, v_ref, o_ref, *, seq):
    j = pl.program_id(1)
    t = pl.program_id(2)
    tq = q_ref.shape[0]
    g = B_HEADS // B_KV_HEADS
    band = tq + 2 * WINDOW
    start = pl.multiple_of(jnp.clip(t * tq - WINDOW, 0, seq - band), LANES)
    kb = k_ref[pl.ds(start, band), :]
    vb = v_ref[pl.ds(start, band), :]
    qs = _stack_heads(q_ref[...])
    s = _dot_nt(qs, kb)
    row = lax.broadcasted_iota(jnp.int32, (g * tq, band), 0)
    kpos = start + lax.broadcasted_iota(jnp.int32, (g * tq, band), 1)
    qpos = t * tq + row % tq
    s = jnp.where(jnp.abs(kpos - qpos) <= WINDOW, s, -jnp.inf)
    hrow = lax.broadcasted_iota(jnp.int32, (g * tq, 1), 0) // tq
    sink = jnp.zeros((g * tq, 1), _F32)
    for h in range(g):
        sink = jnp.where(hrow == h, sink_ref[j * g + h], sink)
    m = jnp.maximum(jnp.max(s, axis=-1, keepdims=True), sink)
    p = jnp.exp(s - m)
    l = jnp.sum(p, axis=-1, keepdims=True) + jnp.exp(sink - m)
    o = _dot(p.astype(_BF16), vb) / l
    o_ref[...] = _unstack_heads(o, tq).astype(o_ref.dtype)


def _attn_b(qkv, sink, *, batch, seq, tq):
    M = qkv.shape[0]
    nt = seq // tq
    gw = (B_HEADS // B_KV_HEADS) * HEAD_DIM
    base = (A_HEADS + 4 * A_KV_HEADS) * HEAD_DIM
    q_col = base // gw
    k_col = (base + B_HEADS * HEAD_DIM) // LANES
    v_col = k_col + B_KV_HEADS
    return pl.pallas_call(
        functools.partial(_attn_b_kernel, seq=seq),
        grid=(batch, B_KV_HEADS, nt),
        in_specs=[
            pl.BlockSpec(memory_space=pltpu.SMEM),
            pl.BlockSpec((tq, gw), lambda b, j, t: (b * nt + t, q_col + j)),
            pl.BlockSpec((seq, LANES), lambda b, j, t: (b, k_col + j)),
            pl.BlockSpec((seq, LANES), lambda b, j, t: (b, v_col + j)),
        ],
        out_specs=pl.BlockSpec((tq, gw), lambda b, j, t: (b * nt + t, j)),
        out_shape=jax.ShapeDtypeStruct((M, B_HEADS * HEAD_DIM), _BF16),
        compiler_params=_params("parallel", "parallel", "parallel"),
        name="attn_b",
    )(sink, qkv, qkv, qkv)


def _attn_c_kernel(q_ref, k_ref, v_ref, bias_ref, o_ref, *, rows):
    band = NA_KH * GRID_W

    def body(r, carry):
        rs = jnp.clip(r - NA_KH // 2, 0, rows - NA_KH)
        q0 = pl.multiple_of(r * GRID_W, GRID_W)
        k0 = pl.multiple_of(rs * GRID_W, GRID_W)
        qs = _stack_heads(q_ref[pl.ds(q0, GRID_W), :])
        s = _dot_nt(qs, k_ref[pl.ds(k0, band), :])
        s = s + bias_ref[rs - r + NA_KH - 1]
        m = jnp.max(s, axis=-1, keepdims=True)
        p = jnp.exp(s - m)
        l = jnp.sum(p, axis=-1, keepdims=True)
        o = _dot(p.astype(_BF16), v_ref[pl.ds(k0, band), :]) / l
        o_ref[pl.ds(q0, GRID_W), :] = _unstack_heads(o, GRID_W).astype(o_ref.dtype)
        return carry

    lax.fori_loop(0, rows, body, 0)


def _attn_c(qkv, bias, *, batch, seq):
    M = qkv.shape[0]
    width = qkv.shape[1] // 3
    npair = width // LANES
    rows = seq // GRID_W
    return pl.pallas_call(
        functools.partial(_attn_c_kernel, rows=rows),
        grid=(batch, npair),
        in_specs=[
            pl.BlockSpec((seq, LANES), lambda b, p: (b, p)),
            pl.BlockSpec((seq, LANES), lambda b, p: (b, npair + p)),
            pl.BlockSpec((seq, LANES), lambda b, p: (b, 2 * npair + p)),
            pl.BlockSpec((None, NA_KH, LANES, NA_KH * GRID_W), lambda b, p: (p, 0, 0, 0)),
        ],
        out_specs=pl.BlockSpec((seq, LANES), lambda b, p: (b, p)),
        out_shape=jax.ShapeDtypeStruct((M, width), _BF16),
        compiler_params=_params("parallel", "parallel"),
        name="attn_c",
    )(qkv, qkv, qkv, bias)


def _c_bias_table(rpb):
    H = rpb.shape[0]
    w = np.arange(GRID_W)
    kc = np.arange(GRID_W)
    cs = np.clip(w - NA_KW // 2, 0, GRID_W - NA_KW)
    inside = (kc[None, :] >= cs[:, None]) & (kc[None, :] < cs[:, None] + NA_KW)
    dc = np.clip(kc[None, :] - w[:, None] + NA_KW - 1, 0, 2 * NA_KW - 2)
    e = np.arange(NA_KH)
    a = np.arange(NA_KH)
    dr = e[:, None] + a[None, :]
    t = rpb[:, dr[:, :, None, None], dc[None, None, :, :]]
    t = jnp.where(inside[None, None, None], t, -jnp.inf)
    t = t.transpose(0, 1, 3, 2, 4).reshape(H // 2, 2, NA_KH, GRID_W, NA_KH * GRID_W)
    return t.transpose(0, 2, 1, 3, 4).reshape(H // 2, NA_KH, 2 * GRID_W, NA_KH * GRID_W)


def _attn_mem_kernel(q_ref, k_ref, v_ref, o_ref):
    hd = q_ref.shape[1] // MEM_HEADS
    for h in range(MEM_HEADS):
        sl = slice(h * hd, (h + 1) * hd)
        s = _dot_nt(q_ref[:, sl], k_ref[:, sl])
        m = jnp.max(s, axis=-1, keepdims=True)
        p = jnp.exp(s - m)
        l = jnp.sum(p, axis=-1, keepdims=True)
        o_ref[:, sl] = (_dot(p.astype(_BF16), v_ref[:, sl]) / l).astype(o_ref.dtype)


def _attn_mem(q, kv, *, batch, seq, tq):
    M, D = q.shape
    mt = kv.shape[0] // batch
    nt = seq // tq
    return pl.pallas_call(
        _attn_mem_kernel,
        grid=(batch, nt),
        in_specs=[
            pl.BlockSpec((tq, D), lambda b, t: (b * nt + t, 0)),
            pl.BlockSpec((mt, D), lambda b, t: (b, 0)),
            pl.BlockSpec((mt, D), lambda b, t: (b, 1)),
        ],
        out_specs=pl.BlockSpec((tq, D), lambda b, t: (b * nt + t, 0)),
        out_shape=jax.ShapeDtypeStruct((M, D), _BF16),
        compiler_params=_params("parallel", "parallel"),
        name="attn_mem",
    )(q, kv, kv)


def _rope_tables(ang):
    half = HEAD_DIM // 2
    cos, sin = jnp.cos(ang), jnp.sin(ang)
    zero = jnp.zeros_like(sin)
    cos_h = jnp.concatenate([cos, cos], axis=-1)
    sin_lo = jnp.concatenate([-sin, zero], axis=-1)
    sin_hi = jnp.concatenate([zero, sin], axis=-1)
    assert cos_h.shape[-1] == 2 * half
    return tuple(jnp.concatenate([t, t], axis=-1).astype(_F32) for t in (cos_h, sin_lo, sin_hi))


def _rope_angles(pos, dim):
    inv = ROPE_THETA ** (-jnp.arange(0, dim, 2, dtype=_F32) / dim)
    return pos[:, None] * inv[None, :]


def _ab_columns():
    d = HEAD_DIM
    cols, c = [], 0
    for n_q, n_kv in ((A_HEADS, A_KV_HEADS), (B_HEADS, B_KV_HEADS)):
        cols += list(range(c, c + n_q * d))
        c += n_q * d
        for _ in range(2):
            for h in range(n_kv):
                cols += 2 * list(range(c + h * d, c + (h + 1) * d))
            c += n_kv * d
    return np.asarray(cols, dtype=np.int32)


def _pick(n, pref):
    t = min(n, pref)
    assert n % t == 0, (n, pref)
    return t


def kernel(x, mem, ln_g, ln_b, ffn_w_gate, ffn_w_up, ffn_w_down, ab_w_in, ab_w_out, ab_q_gain, ab_k_gain, ab_sink,
           c_w_in, c_w_out, c_rpb, mem_w_q, mem_w_kv, mem_w_o):
    B, S, D = x.shape
    depth = ln_g.shape[0]
    M = B * S
    MT = mem.shape[1]
    rows = S // GRID_W
    assert S % GRID_W == 0 and rows >= NA_KH and S % Q_BLOCK == 0 and S >= Q_BLOCK + 2 * WINDOW
    assert D % (MEM_HEADS * LANES) == 0 and D == 2 * A_HEADS * HEAD_DIM
    alpha = (2.0 * depth) ** 0.25

    t = jnp.arange(S)
    row = (t // GRID_W).astype(_F32)
    colp = (t % GRID_W).astype(_F32)
    ang_2d = jnp.concatenate([_rope_angles(row, HEAD_DIM // 2), _rope_angles(colp, HEAD_DIM // 2)], axis=-1)
    ang_1d = _rope_angles(t.astype(_F32), HEAD_DIM)
    tabs2 = _rope_tables(ang_2d)
    tabs1 = _rope_tables(ang_1d)
    lane_head = np.arange(LANES) // HEAD_DIM
    bd = jnp.asarray((lane_head[:, None] == lane_head[None, :]) / HEAD_DIM, dtype=_BF16)

    wg = ffn_w_gate.astype(_BF16)
    wu = ffn_w_up.astype(_BF16)
    wd = ffn_w_down.astype(_BF16)
    ab_cols = _ab_columns()

    tm_ffn = _pick(M, 1024)
    tf = _pick(wg.shape[-1], 256)
    tm = _pick(M, 512)
    tm_ab = _pick(S, 512)

    xf = x.reshape(M, D)
    memf = mem.reshape(B * MT, D)
    for i in range(depth):
        j = i // 2
        g = lambda k: ln_g[i, k].reshape(1, D)
        bb = lambda k: ln_b[i, k].reshape(1, D)
        xf = _ffn_ln(xf, wg, wu, wd, g(0), bb(0), i, 0, alpha=alpha, tm=tm_ffn, tf=tf)
        if i % 2 == 0:
            w_ext = ab_w_in[j][:, ab_cols].astype(_BF16)
            qg = jnp.tile(ab_q_gain[j], LANES // HEAD_DIM).reshape(1, LANES)
            kg = jnp.tile(ab_k_gain[j], LANES // HEAD_DIM).reshape(1, LANES)
            qkv = _ab_proj(xf, w_ext, bd, qg, kg, tabs2, tabs1, seq=S, tm=tm_ab)
            out_a = _attn_a(qkv, batch=B, seq=S, tq=_pick(S, 128))
            out_b = _attn_b(qkv, ab_sink[j], batch=B, seq=S, tq=Q_BLOCK)
            w_out = ab_w_out[j].astype(_BF16)
            na = A_HEADS * HEAD_DIM
            xf = _out_ln([out_a, out_b], [w_out[:na], w_out[na:]], xf, g(1), bb(1), alpha=alpha, tm=tm)
        else:
            width = c_w_in.shape[-1] // 3
            scale_cols = jnp.where(jnp.arange(3 * width) < width, HEAD_DIM ** -0.5, 1.0).astype(_F32)
            qkv = _proj(xf, (c_w_in[j] * scale_cols[None, :]).astype(_BF16), tm=tm, tn=_pick(3 * width, 1024))
            out_c = _attn_c(qkv, _c_bias_table(c_rpb[j]), batch=B, seq=S)
            xf = _out_ln([out_c], [c_w_out[j].astype(_BF16)], xf, g(1), bb(1), alpha=alpha, tm=tm)
        hd = D // MEM_HEADS
        q = _proj(xf, mem_w_q[i].astype(_BF16), scale=hd ** -0.5, tm=tm, tn=_pick(D, 1024))
        kv = _proj(memf, mem_w_kv[i].astype(_BF16), tm=_pick(B * MT, 512), tn=_pick(2 * D, 1024))
        o = _attn_mem(q, kv, batch=B, seq=S, tq=_pick(S, 512))
        xf = _out_ln([o], [mem_w_o[i].astype(_BF16)], xf, g(2), bb(2), alpha=alpha, tm=tm)
        xf = _ffn_ln(xf, wg, wu, wd, g(3), bb(3), i, 1, alpha=alpha, tm=tm_ffn, tf=tf)
    return xf.reshape(B, S, D)
```

```python
import functools

import numpy as np
import jax
import jax.numpy as jnp
from jax import lax
from jax.experimental import pallas as pl
from jax.experimental.pallas import tpu as pltpu

GRID_W = 64
HEAD_DIM = 64
A_HEADS = 8
A_KV_HEADS = 2
B_HEADS = 8
B_KV_HEADS = 2
Q_BLOCK = 128
WINDOW = 128
NA_KH = 8
NA_KW = 16
MEM_HEADS = 4
ROPE_THETA = 10000.0
LN_EPS = 1e-5
RMS_EPS = 1e-6

LANES = 128
VMEM_LIMIT_BYTES = 48 * 1024 * 1024

_BF16 = jnp.bfloat16
_F32 = jnp.float32
_NT = (((1,), (1,)), ((), ()))


def _dot(a, b):
    return jnp.dot(a, b, preferred_element_type=_F32)


def _dot_nt(a, b):
    return lax.dot_general(a, b, _NT, preferred_element_type=_F32)


def _params(*sem):
    return pltpu.CompilerParams(dimension_semantics=sem, vmem_limit_bytes=VMEM_LIMIT_BYTES)


def _layer_norm(z, g, b):
    mu = jnp.mean(z, axis=-1, keepdims=True)
    zc = z - mu
    var = jnp.mean(zc * zc, axis=-1, keepdims=True)
    return zc * lax.rsqrt(var + LN_EPS) * g + b


def _ffn_ln_kernel(x_ref, wg_ref, wu_ref, wd_ref, g_ref, b_ref, o_ref, act_ref, *, alpha, tf):
    xb = x_ref[...].astype(_BF16)
    for c in range(0, wg_ref.shape[1], tf):
        gate = _dot(xb, wg_ref[:, c:c + tf])
        up = _dot(xb, wu_ref[:, c:c + tf])
        act_ref[:, c:c + tf] = (gate * jax.nn.sigmoid(gate) * up).astype(_BF16)
    y = _dot(act_ref[...], wd_ref[...])
    o_ref[...] = _layer_norm(alpha * x_ref[...] + 0.5 * y, g_ref[...], b_ref[...])


def _ffn_ln(x, wg, wu, wd, g, b, layer, half, *, alpha, tm, tf):
    M, D = x.shape
    F = wg.shape[-1]
    resident = pl.Buffered(1)
    return pl.pallas_call(
        functools.partial(_ffn_ln_kernel, alpha=alpha, tf=tf),
        grid=(M // tm,),
        in_specs=[
            pl.BlockSpec((tm, D), lambda i: (i, 0)),
            pl.BlockSpec((None, None, D, F), lambda i: (layer, half, 0, 0), pipeline_mode=resident),
            pl.BlockSpec((None, None, D, F), lambda i: (layer, half, 0, 0), pipeline_mode=resident),
            pl.BlockSpec((None, None, F, D), lambda i: (layer, half, 0, 0), pipeline_mode=resident),
            pl.BlockSpec((1, D), lambda i: (0, 0)),
            pl.BlockSpec((1, D), lambda i: (0, 0)),
        ],
        out_specs=pl.BlockSpec((tm, D), lambda i: (i, 0)),
        out_shape=jax.ShapeDtypeStruct((M, D), _F32),
        scratch_shapes=[pltpu.VMEM((tm, F), _BF16)],
        compiler_params=_params("parallel"),
        name="ffn_ln",
    )(x, wg, wu, wd, g, b)


def _proj_kernel(x_ref, w_ref, o_ref, *, scale):
    y = _dot(x_ref[...].astype(_BF16), w_ref[...])
    if scale != 1.0:
        y = y * scale
    o_ref[...] = y.astype(o_ref.dtype)


def _proj(x, w, *, scale=1.0, tm, tn):
    M, K = x.shape
    N = w.shape[1]
    return pl.pallas_call(
        functools.partial(_proj_kernel, scale=scale),
        grid=(M // tm, N // tn),
        in_specs=[pl.BlockSpec((tm, K), lambda i, j: (i, 0)), pl.BlockSpec((K, tn), lambda i, j: (0, j))],
        out_specs=pl.BlockSpec((tm, tn), lambda i, j: (i, j)),
        out_shape=jax.ShapeDtypeStruct((M, N), _BF16),
        compiler_params=_params("parallel", "parallel"),
        name="proj",
    )(x, w)


def _out_ln_kernel(*refs, alpha, n_in):
    a_refs = refs[:n_in]
    w_refs = refs[n_in:2 * n_in]
    x_ref, g_ref, b_ref, o_ref = refs[2 * n_in:]
    y = _dot(a_refs[0][...], w_refs[0][...])
    for a_ref, w_ref in zip(a_refs[1:], w_refs[1:]):
        y = y + _dot(a_ref[...], w_ref[...])
    o_ref[...] = _layer_norm(alpha * x_ref[...] + y, g_ref[...], b_ref[...])


def _out_ln(acts, ws, x, g, b, *, alpha, tm):
    M, D = x.shape
    n_in = len(acts)
    in_specs = [pl.BlockSpec((tm, a.shape[1]), lambda i: (i, 0)) for a in acts]
    in_specs += [pl.BlockSpec(w.shape, lambda i: (0, 0)) for w in ws]
    in_specs += [pl.BlockSpec((tm, D), lambda i: (i, 0)), pl.BlockSpec((1, D), lambda i: (0, 0)),
                 pl.BlockSpec((1, D), lambda i: (0, 0))]
    return pl.pallas_call(
        functools.partial(_out_ln_kernel, alpha=alpha, n_in=n_in),
        grid=(M // tm,),
        in_specs=in_specs,
        out_specs=pl.BlockSpec((tm, D), lambda i: (i, 0)),
        out_shape=jax.ShapeDtypeStruct((M, D), _F32),
        compiler_params=_params("parallel"),
        name="out_ln",
    )(*acts, *ws, x, g, b)


_AB_CHUNK = 256
_AB_LAYOUT = (("qa", 512), ("ka", 256), ("va", 256), ("qb", 512), ("kb", 256), ("vb", 256))


def _rope(y, cos, sin_lo, sin_hi):
    return y * cos + pltpu.roll(y, 96, axis=1) * sin_lo + pltpu.roll(y, 32, axis=1) * sin_hi


def _head_mean_square(y, bd):
    sq = y * y
    hi = sq.astype(_BF16)
    lo = (sq - hi.astype(_F32)).astype(_BF16)
    return _dot(hi, bd) + _dot(lo, bd)


def _ab_proj_kernel(x_ref, w_ref, bd_ref, qg_ref, kg_ref, c2_ref, sl2_ref, sh2_ref, c1_ref, sl1_ref, sh1_ref,
                    o_ref, *, scale):
    xb = x_ref[...].astype(_BF16)
    bd = bd_ref[...]
    col = 0
    for name, width in _AB_LAYOUT:
        for c in range(col, col + width, _AB_CHUNK):
            y2 = _dot(xb, w_ref[:, c:c + _AB_CHUNK])
            for h in range(0, _AB_CHUNK, LANES):
                y = y2[:, h:h + LANES]
                if name in ("qa", "ka"):
                    gain = qg_ref[...] if name == "qa" else kg_ref[...]
                    y = y * lax.rsqrt(_head_mean_square(y, bd) + RMS_EPS) * gain
                    y = _rope(y, c2_ref[...], sl2_ref[...], sh2_ref[...])
                elif name in ("qb", "kb"):
                    y = _rope(y, c1_ref[...], sl1_ref[...], sh1_ref[...])
                if name in ("qa", "qb"):
                    y = y * scale
                o_ref[:, c + h:c + h + LANES] = y.astype(o_ref.dtype)
        col += width


def _ab_proj(x, w_ext, bd, qg, kg, tabs2, tabs1, *, seq, tm):
    M, D = x.shape
    N = w_ext.shape[1]
    nt = seq // tm
    tab_spec = pl.BlockSpec((tm, LANES), lambda i: (i % nt, 0))
    const = lambda shape: pl.BlockSpec(shape, lambda i: (0, 0))
    return pl.pallas_call(
        functools.partial(_ab_proj_kernel, scale=HEAD_DIM ** -0.5),
        grid=(M // tm,),
        in_specs=[pl.BlockSpec((tm, D), lambda i: (i, 0)), const((D, N)), const((LANES, LANES)),
                  const((1, LANES)), const((1, LANES))] + [tab_spec] * 6,
        out_specs=pl.BlockSpec((tm, N), lambda i: (i, 0)),
        out_shape=jax.ShapeDtypeStruct((M, N), _BF16),
        compiler_params=_params("parallel"),
        name="ab_proj",
    )(x, w_ext, bd, qg, kg, *tabs2, *tabs1)


def _stack_heads(q):
    t = q.shape[0]
    lo = lax.broadcasted_iota(jnp.int32, (t, LANES), 1) < HEAD_DIM
    zero = jnp.zeros((t, LANES), q.dtype)
    parts = []
    for c in range(0, q.shape[1], LANES):
        pair = q[:, c:c + LANES]
        parts += [jnp.where(lo, pair, zero), jnp.where(lo, zero, pair)]
    return jnp.concatenate(parts, axis=0)


def _unstack_heads(o, t):
    lo = lax.broadcasted_iota(jnp.int32, (t, LANES), 1) < HEAD_DIM
    n = o.shape[0] // t
    return jnp.concatenate(
        [jnp.where(lo, o[(2 * i) * t:(2 * i + 1) * t], o[(2 * i + 1) * t:(2 * i + 2) * t]) for i in range(n // 2)],
        axis=1)


def _attn_a_kernel(q_ref, k_ref, v_ref, o_ref, *, kc):
    tq = q_ref.shape[0]
    qs = _stack_heads(q_ref[...])
    m = l = acc = None
    for c in range(0, k_ref.shape[0], kc):
        s = _dot_nt(qs, k_ref[c:c + kc, :])
        mc = jnp.max(s, axis=-1, keepdims=True)
        m_new = mc if m is None else jnp.maximum(m, mc)
        p = jnp.exp(s - m_new)
        lc = jnp.sum(p, axis=-1, keepdims=True)
        pv = _dot(p.astype(_BF16), v_ref[c:c + kc, :])
        if m is None:
            l, acc = lc, pv
        else:
            corr = jnp.exp(m - m_new)
            l, acc = l * corr + lc, acc * corr + pv
        m = m_new
    o_ref[...] = _unstack_heads(acc / l, tq).astype(o_ref.dtype)


def _attn_a(qkv, *, batch, seq, tq, kc):
    M = qkv.shape[0]
    nt = seq // tq
    gw = (A_HEADS // A_KV_HEADS) * HEAD_DIM
    k_col = A_HEADS * HEAD_DIM // LANES
    v_col = k_col + A_KV_HEADS
    return pl.pallas_call(
        functools.partial(_attn_a_kernel, kc=kc),
        grid=(batch, A_KV_HEADS, nt),
        in_specs=[
            pl.BlockSpec((tq, gw), lambda b, j, t: (b * nt + t, j)),
            pl.BlockSpec((seq, LANES), lambda b, j, t: (b, k_col + j)),
            pl.BlockSpec((seq, LANES), lambda b, j, t: (b, v_col + j)),
        ],
        out_specs=pl.BlockSpec((tq, gw), lambda b, j, t: (b * nt + t, j)),
        out_shape=jax.ShapeDtypeStruct((M, A_HEADS * HEAD_DIM), _BF16),
        compiler_params=_params("parallel", "parallel", "parallel"),
        name="attn_a",
    )(qkv, qkv, qkv)


def _attn_b_kernel(sink_ref, q_ref, k_ref, v_ref, o_ref, *, seq):
    j = pl.program_id(1)
    t = pl.program_id(2)
    qb = Q_BLOCK
    g = B_HEADS // B_KV_HEADS
    band = qb + 2 * WINDOW
    row = lax.broadcasted_iota(jnp.int32, (g * qb, band), 0)
    col = lax.broadcasted_iota(jnp.int32, (g * qb, band), 1)
    rel = col - jnp.bitwise_and(row, qb - 1)
    hrow = lax.broadcasted_iota(jnp.int32, (g * qb, 1), 0) // qb
    sink = jnp.zeros((g * qb, 1), _F32)
    for h in range(g):
        sink = jnp.where(hrow == h, sink_ref[j * g + h], sink)
    for u in range(q_ref.shape[0] // qb):
        q0 = (t * (q_ref.shape[0] // qb) + u) * qb
        start = pl.multiple_of(jnp.clip(q0 - WINDOW, 0, seq - band), LANES)
        kb = k_ref[pl.ds(start, band), :]
        vb = v_ref[pl.ds(start, band), :]
        qs = _stack_heads(q_ref[u * qb:(u + 1) * qb, :])
        s = _dot_nt(qs, kb)
        s = jnp.where(jnp.abs(rel + (start - q0)) <= WINDOW, s, -jnp.inf)
        m = jnp.maximum(jnp.max(s, axis=-1, keepdims=True), sink)
        p = jnp.exp(s - m)
        l = jnp.sum(p, axis=-1, keepdims=True) + jnp.exp(sink - m)
        o = _dot(p.astype(_BF16), vb) / l
        o_ref[u * qb:(u + 1) * qb, :] = _unstack_heads(o, qb).astype(o_ref.dtype)


def _attn_b(qkv, sink, *, batch, seq, tq):
    M = qkv.shape[0]
    nt = seq // tq
    assert tq % Q_BLOCK == 0 and Q_BLOCK & (Q_BLOCK - 1) == 0
    gw = (B_HEADS // B_KV_HEADS) * HEAD_DIM
    base = (A_HEADS + 4 * A_KV_HEADS) * HEAD_DIM
    q_col = base // gw
    k_col = (base + B_HEADS * HEAD_DIM) // LANES
    v_col = k_col + B_KV_HEADS
    return pl.pallas_call(
        functools.partial(_attn_b_kernel, seq=seq),
        grid=(batch, B_KV_HEADS, nt),
        in_specs=[
            pl.BlockSpec(memory_space=pltpu.SMEM),
            pl.BlockSpec((tq, gw), lambda b, j, t: (b * nt + t, q_col + j)),
            pl.BlockSpec((seq, LANES), lambda b, j, t: (b, k_col + j)),
            pl.BlockSpec((seq, LANES), lambda b, j, t: (b, v_col + j)),
        ],
        out_specs=pl.BlockSpec((tq, gw), lambda b, j, t: (b * nt + t, j)),
        out_shape=jax.ShapeDtypeStruct((M, B_HEADS * HEAD_DIM), _BF16),
        compiler_params=_params("parallel", "parallel", "parallel"),
        name="attn_b",
    )(sink, qkv, qkv, qkv)


C_GROUP = 4
C_BAND = C_GROUP + NA_KH
_C_GROUP_UNROLL = 8


def _c_band_start(g, rows):
    return jnp.clip(g * C_GROUP - NA_KH // 2, 0, rows - C_BAND)


def _attn_c_kernel(q_ref, k_ref, v_ref, bias_ref, o_ref, *, rows):
    nq = C_GROUP * GRID_W
    nk = C_BAND * GRID_W
    groups = rows // C_GROUP

    def body(g, carry):
        q0 = pl.multiple_of(g * nq, nq)
        k0 = pl.multiple_of(_c_band_start(g, rows) * GRID_W, GRID_W)
        kind = jnp.where(g == 0, 0, jnp.where(g == groups - 1, 2, 1))
        qs = _stack_heads(q_ref[pl.ds(q0, nq), :])
        s = _dot_nt(qs, k_ref[pl.ds(k0, nk), :])
        s = s + bias_ref[kind]
        m = jnp.max(s, axis=-1, keepdims=True)
        p = jnp.exp(s - m)
        l = jnp.sum(p, axis=-1, keepdims=True)
        o = _dot(p.astype(_BF16), v_ref[pl.ds(k0, nk), :]) / l
        o_ref[pl.ds(q0, nq), :] = _unstack_heads(o, nq).astype(o_ref.dtype)
        return carry

    lax.fori_loop(0, groups, body, 0, unroll=_C_GROUP_UNROLL)


def _attn_c(qkv, bias, *, batch, seq):
    M = qkv.shape[0]
    width = qkv.shape[1] // 3
    npair = width // LANES
    rows = seq // GRID_W
    return pl.pallas_call(
        functools.partial(_attn_c_kernel, rows=rows),
        grid=(batch, npair),
        in_specs=[
            pl.BlockSpec((seq, LANES), lambda b, p: (b, p)),
            pl.BlockSpec((seq, LANES), lambda b, p: (b, npair + p)),
            pl.BlockSpec((seq, LANES), lambda b, p: (b, 2 * npair + p)),
            pl.BlockSpec((None,) + bias.shape[1:], lambda b, p: (p, 0, 0, 0)),
        ],
        out_specs=pl.BlockSpec((seq, LANES), lambda b, p: (b, p)),
        out_shape=jax.ShapeDtypeStruct((M, width), _BF16),
        compiler_params=_params("parallel", "parallel"),
        name="attn_c",
    )(qkv, qkv, qkv, bias)


def _c_window(kind, u):
    if kind == 0:
        return 0, NA_KH, 0
    if kind == 1:
        return u, u + NA_KH, -(NA_KH // 2)
    return C_GROUP, C_GROUP + NA_KH, C_GROUP - C_BAND


def _c_bias_kernel(rpb_ref, o_ref):
    nq = C_GROUP * GRID_W
    w = lax.broadcasted_iota(jnp.int32, (GRID_W, LANES), 0)
    lane = lax.broadcasted_iota(jnp.int32, (GRID_W, LANES), 1)
    kc = jnp.bitwise_and(lane, GRID_W - 1)
    upper = lane >= GRID_W
    cs = jnp.clip(w - NA_KW // 2, 0, GRID_W - NA_KW)
    inside = (kc >= cs) & (kc < cs + NA_KW)
    for hh in range(2):
        lo_t, hi_t = [], []
        for i in range(2 * NA_KH - 1):
            r = jnp.broadcast_to(rpb_ref[hh, i:i + 1, :], (GRID_W, LANES))
            lo_t.append(pltpu.roll(r, LANES - (NA_KW - 1), 1, stride=1, stride_axis=0))
            hi_t.append(pltpu.roll(r, GRID_W - (NA_KW - 1), 1, stride=1, stride_axis=0))
        neg = jnp.full((GRID_W, LANES), -jnp.inf, _F32)
        for kind in range(3):
            for u in range(C_GROUP):
                lo, hi, shift = _c_window(kind, u)
                for b2 in range(C_BAND // 2):
                    halves = []
                    for b, table in ((2 * b2, lo_t), (2 * b2 + 1, hi_t)):
                        halves.append(table[shift + b - u + NA_KH - 1] if lo <= b < hi else neg)
                    tile = jnp.where(inside, jnp.where(upper, halves[1], halves[0]), neg)
                    r0 = hh * nq + u * GRID_W
                    o_ref[kind, r0:r0 + GRID_W, b2 * LANES:(b2 + 1) * LANES] = tile


def _c_bias_table(rpb):
    H, nr, nc = rpb.shape
    assert nr == 2 * NA_KH - 1 and nc == 2 * NA_KW - 1 and 2 * GRID_W == LANES and H % 2 == 0 and C_BAND % 2 == 0
    rpb_pad = jnp.pad(rpb, ((0, 0), (0, 0), (0, LANES - nc)))
    shape = (3, 2 * C_GROUP * GRID_W, C_BAND * GRID_W)
    return pl.pallas_call(
        _c_bias_kernel,
        grid=(H // 2,),
        in_specs=[pl.BlockSpec((2, nr, LANES), lambda p: (p, 0, 0))],
        out_specs=pl.BlockSpec((None,) + shape, lambda p: (p, 0, 0, 0)),
        out_shape=jax.ShapeDtypeStruct((H // 2,) + shape, _F32),
        compiler_params=_params("parallel"),
        name="c_bias",
    )(rpb_pad)


def _attn_mem_kernel(q_ref, k_ref, v_ref, o_ref):
    hd = q_ref.shape[1] // MEM_HEADS
    for h in range(MEM_HEADS):
        sl = slice(h * hd, (h + 1) * hd)
        s = _dot_nt(q_ref[:, sl], k_ref[:, sl])
        m = jnp.max(s, axis=-1, keepdims=True)
        p = jnp.exp(s - m)
        l = jnp.sum(p, axis=-1, keepdims=True)
        o_ref[:, sl] = (_dot(p.astype(_BF16), v_ref[:, sl]) / l).astype(o_ref.dtype)


def _attn_mem(q, kv, *, batch, seq, tq):
    M, D = q.shape
    mt = kv.shape[0] // batch
    nt = seq // tq
    return pl.pallas_call(
        _attn_mem_kernel,
        grid=(batch, nt),
        in_specs=[
            pl.BlockSpec((tq, D), lambda b, t: (b * nt + t, 0)),
            pl.BlockSpec((mt, D), lambda b, t: (b, 0)),
            pl.BlockSpec((mt, D), lambda b, t: (b, 1)),
        ],
        out_specs=pl.BlockSpec((tq, D), lambda b, t: (b * nt + t, 0)),
        out_shape=jax.ShapeDtypeStruct((M, D), _BF16),
        compiler_params=_params("parallel", "parallel"),
        name="attn_mem",
    )(q, kv, kv)


def _rope_tables(ang):
    half = HEAD_DIM // 2
    cos, sin = jnp.cos(ang), jnp.sin(ang)
    zero = jnp.zeros_like(sin)
    cos_h = jnp.concatenate([cos, cos], axis=-1)
    sin_lo = jnp.concatenate([-sin, zero], axis=-1)
    sin_hi = jnp.concatenate([zero, sin], axis=-1)
    assert cos_h.shape[-1] == 2 * half
    return tuple(jnp.concatenate([t, t], axis=-1).astype(_F32) for t in (cos_h, sin_lo, sin_hi))


def _rope_angles(pos, dim):
    inv = ROPE_THETA ** (-jnp.arange(0, dim, 2, dtype=_F32) / dim)
    return pos[:, None] * inv[None, :]


def _ab_columns():
    d = HEAD_DIM
    cols, c = [], 0
    for n_q, n_kv in ((A_HEADS, A_KV_HEADS), (B_HEADS, B_KV_HEADS)):
        cols += list(range(c, c + n_q * d))
        c += n_q * d
        for _ in range(2):
            for h in range(n_kv):
                cols += 2 * list(range(c + h * d, c + (h + 1) * d))
            c += n_kv * d
    return np.asarray(cols, dtype=np.int32)


def _pick(n, pref):
    t = min(n, pref)
    assert n % t == 0, (n, pref)
    return t


def kernel(x, mem, ln_g, ln_b, ffn_w_gate, ffn_w_up, ffn_w_down, ab_w_in, ab_w_out, ab_q_gain, ab_k_gain, ab_sink,
           c_w_in, c_w_out, c_rpb, mem_w_q, mem_w_kv, mem_w_o):
    B, S, D = x.shape
    depth = ln_g.shape[0]
    M = B * S
    MT = mem.shape[1]
    rows = S // GRID_W
    assert S % GRID_W == 0 and rows >= C_BAND and rows % C_GROUP == 0 and NA_KH // 2 == C_GROUP
    assert S % Q_BLOCK == 0 and S >= Q_BLOCK + 2 * WINDOW
    assert D % (MEM_HEADS * LANES) == 0 and D == 2 * A_HEADS * HEAD_DIM
    alpha = (2.0 * depth) ** 0.25

    t = jnp.arange(S)
    row = (t // GRID_W).astype(_F32)
    colp = (t % GRID_W).astype(_F32)
    ang_2d = jnp.concatenate([_rope_angles(row, HEAD_DIM // 2), _rope_angles(colp, HEAD_DIM // 2)], axis=-1)
    ang_1d = _rope_angles(t.astype(_F32), HEAD_DIM)
    tabs2 = _rope_tables(ang_2d)
    tabs1 = _rope_tables(ang_1d)
    lane_head = np.arange(LANES) // HEAD_DIM
    bd = jnp.asarray((lane_head[:, None] == lane_head[None, :]) / HEAD_DIM, dtype=_BF16)

    wg = ffn_w_gate.astype(_BF16)
    wu = ffn_w_up.astype(_BF16)
    wd = ffn_w_down.astype(_BF16)
    ab_cols = _ab_columns()

    tm_ffn = _pick(M, 512)
    tf = _pick(wg.shape[-1], 256)
    tm = _pick(M, 512)
    tm_ab = _pick(S, 512)

    xf = x.reshape(M, D)
    memf = mem.reshape(B * MT, D)
    for i in range(depth):
        j = i // 2
        g = lambda k: ln_g[i, k].reshape(1, D)
        bb = lambda k: ln_b[i, k].reshape(1, D)
        xf = _ffn_ln(xf, wg, wu, wd, g(0), bb(0), i, 0, alpha=alpha, tm=tm_ffn, tf=tf)
        if i % 2 == 0:
            w_ext = ab_w_in[j][:, ab_cols].astype(_BF16)
            qg = jnp.tile(ab_q_gain[j], LANES // HEAD_DIM).reshape(1, LANES)
            kg = jnp.tile(ab_k_gain[j], LANES // HEAD_DIM).reshape(1, LANES)
            qkv = _ab_proj(xf, w_ext, bd, qg, kg, tabs2, tabs1, seq=S, tm=tm_ab)
            out_a = _attn_a(qkv, batch=B, seq=S, tq=_pick(S, 128), kc=_pick(S, 512))
            out_b = _attn_b(qkv, ab_sink[j], batch=B, seq=S, tq=_pick(S, 4 * Q_BLOCK))
            w_out = ab_w_out[j].astype(_BF16)
            na = A_HEADS * HEAD_DIM
            xf = _out_ln([out_a, out_b], [w_out[:na], w_out[na:]], xf, g(1), bb(1), alpha=alpha, tm=tm)
        else:
            width = c_w_in.shape[-1] // 3
            scale_cols = jnp.where(jnp.arange(3 * width) < width, HEAD_DIM ** -0.5, 1.0).astype(_F32)
            qkv = _proj(xf, (c_w_in[j] * scale_cols[None, :]).astype(_BF16), tm=tm, tn=_pick(3 * width, 1024))
            out_c = _attn_c(qkv, _c_bias_table(c_rpb[j]), batch=B, seq=S)
            xf = _out_ln([out_c], [c_w_out[j].astype(_BF16)], xf, g(1), bb(1), alpha=alpha, tm=tm)
        hd = D // MEM_HEADS
        q = _proj(xf, mem_w_q[i].astype(_BF16), scale=hd ** -0.5, tm=tm, tn=_pick(D, 1024))
        kv = _proj(memf, mem_w_kv[i].astype(_BF16), tm=_pick(B * MT, 512), tn=_pick(2 * D, 1024))
        o = _attn_mem(q, kv, batch=B, seq=S, tq=_pick(S, 512))
        xf = _out_ln([o], [mem_w_o[i].astype(_BF16)], xf, g(2), bb(2), alpha=alpha, tm=tm)
        xf = _ffn_ln(xf, wg, wu, wd, g(3), bb(3), i, 1, alpha=alpha, tm=tm_ffn, tf=tf)
    return xf.reshape(B, S, D)
```

```python
import functools

import numpy as np
import jax
import jax.numpy as jnp
from jax import lax
from jax.experimental import pallas as pl
from jax.experimental.pallas import tpu as pltpu

GRID_W = 64
HEAD_DIM = 64
A_HEADS = 8
A_KV_HEADS = 2
B_HEADS = 8
B_KV_HEADS = 2
Q_BLOCK = 128
WINDOW = 128
NA_KH = 8
NA_KW = 16
MEM_HEADS = 4
ROPE_THETA = 10000.0
LN_EPS = 1e-5
RMS_EPS = 1e-6

LANES = 128
VMEM_LIMIT_BYTES = 48 * 1024 * 1024

_BF16 = jnp.bfloat16
_F32 = jnp.float32
_NT = (((1,), (1,)), ((), ()))
_LOG2E = 1.4426950408889634


def _dot(a, b):
    return jnp.dot(a, b, preferred_element_type=_F32)


def _dot_nt(a, b):
    return lax.dot_general(a, b, _NT, preferred_element_type=_F32)


def _params(*sem):
    return pltpu.CompilerParams(dimension_semantics=sem, vmem_limit_bytes=VMEM_LIMIT_BYTES)


def _layer_norm(z, g, b):
    mu = jnp.mean(z, axis=-1, keepdims=True)
    zc = z - mu
    var = jnp.mean(zc * zc, axis=-1, keepdims=True)
    return zc * lax.rsqrt(var + LN_EPS) * g + b


_FFN_OUT_SLABS = 2


def _ffn_ln_kernel(x_ref, wg_ref, wu_ref, wd_ref, g_ref, b_ref, o_ref, act_ref, *, alpha, tf):
    xb = x_ref[...].astype(_BF16)
    for c in range(0, wg_ref.shape[1], tf):
        gate = _dot(xb, wg_ref[:, c:c + tf])
        up = _dot(xb, wu_ref[:, c:c + tf])
        act_ref[:, c:c + tf] = (gate * jax.nn.sigmoid(gate) * up).astype(_BF16)
    slab = x_ref.shape[0] // _FFN_OUT_SLABS
    for r in range(0, x_ref.shape[0], slab):
        y = _dot(act_ref[r:r + slab, :], wd_ref[...])
        o_ref[r:r + slab, :] = _layer_norm(alpha * x_ref[r:r + slab, :] + 0.5 * y, g_ref[...], b_ref[...])


def _ffn_ln(x, wg, wu, wd, g, b, layer, half, *, alpha, tm, tf):
    M, D = x.shape
    F = wg.shape[-1]
    resident = pl.Buffered(1)
    return pl.pallas_call(
        functools.partial(_ffn_ln_kernel, alpha=alpha, tf=tf),
        grid=(M // tm,),
        in_specs=[
            pl.BlockSpec((tm, D), lambda i: (i, 0)),
            pl.BlockSpec((None, None, D, F), lambda i: (layer, half, 0, 0), pipeline_mode=resident),
            pl.BlockSpec((None, None, D, F), lambda i: (layer, half, 0, 0), pipeline_mode=resident),
            pl.BlockSpec((None, None, F, D), lambda i: (layer, half, 0, 0), pipeline_mode=resident),
            pl.BlockSpec((1, D), lambda i: (0, 0)),
            pl.BlockSpec((1, D), lambda i: (0, 0)),
        ],
        out_specs=pl.BlockSpec((tm, D), lambda i: (i, 0)),
        out_shape=jax.ShapeDtypeStruct((M, D), _F32),
        scratch_shapes=[pltpu.VMEM((tm, F), _BF16)],
        compiler_params=_params("parallel"),
        name="ffn_ln",
    )(x, wg, wu, wd, g, b)


def _proj_kernel(x_ref, w_ref, o_ref, *, tn):
    xb = x_ref[...].astype(_BF16)
    for c in range(0, w_ref.shape[1], tn):
        o_ref[:, c:c + tn] = _dot(xb, w_ref[:, c:c + tn]).astype(o_ref.dtype)


def _proj(x, w, *, tm, tn):
    M, K = x.shape
    N = w.shape[1]
    return pl.pallas_call(
        functools.partial(_proj_kernel, tn=tn),
        grid=(M // tm,),
        in_specs=[pl.BlockSpec((tm, K), lambda i: (i, 0)),
                  pl.BlockSpec((K, N), lambda i: (0, 0), pipeline_mode=pl.Buffered(1))],
        out_specs=pl.BlockSpec((tm, N), lambda i: (i, 0)),
        out_shape=jax.ShapeDtypeStruct((M, N), _BF16),
        compiler_params=_params("parallel"),
        name="proj",
    )(x, w)


def _softmax_pv(s, v):
    m = jnp.max(s, axis=-1, keepdims=True)
    p = jnp.exp(s - m)
    l = jnp.sum(p, axis=-1, keepdims=True)
    return _dot(p.astype(_BF16), v) / l


def _mix_mem_kernel(*refs, alpha, n_in, scale):
    a_refs = refs[:n_in]
    w_refs = refs[n_in:2 * n_in]
    x_ref, g1_ref, b1_ref, wq_ref, k_ref, v_ref, wo_ref, g2_ref, b2_ref, o_ref = refs[2 * n_in:]
    y = _dot(a_refs[0][...], w_refs[0][...])
    for a_ref, w_ref in zip(a_refs[1:], w_refs[1:]):
        y = y + _dot(a_ref[...], w_ref[...])
    x1 = _layer_norm(alpha * x_ref[...] + y, g1_ref[...], b1_ref[...])
    q = (_dot(x1.astype(_BF16), wq_ref[...]) * scale).astype(_BF16)
    hd = q.shape[1] // MEM_HEADS
    heads = []
    for h in range(MEM_HEADS):
        sl = slice(h * hd, (h + 1) * hd)
        heads.append(_softmax_pv(_dot_nt(q[:, sl], k_ref[:, sl]), v_ref[:, sl]).astype(_BF16))
    y2 = _dot(jnp.concatenate(heads, axis=1), wo_ref[...])
    o_ref[...] = _layer_norm(alpha * x1 + y2, g2_ref[...], b2_ref[...])


def _mix_mem(acts, ws, x, g1, b1, wq, kv, wo, g2, b2, *, alpha, batch, seq, tm):
    M, D = x.shape
    n_in = len(acts)
    mt = kv.shape[0] // batch
    nt = seq // tm
    row = lambda b, t: (b * nt + t, 0)
    const = lambda b, t: (0, 0)
    resident = pl.Buffered(1)
    in_specs = [pl.BlockSpec((tm, a.shape[1]), row) for a in acts]
    in_specs += [pl.BlockSpec(w.shape, const, pipeline_mode=resident) for w in ws]
    in_specs += [
        pl.BlockSpec((tm, D), row), pl.BlockSpec((1, D), const), pl.BlockSpec((1, D), const),
        pl.BlockSpec((D, D), const, pipeline_mode=resident),
        pl.BlockSpec((mt, D), lambda b, t: (b, 0)),
        pl.BlockSpec((mt, D), lambda b, t: (b, 1)),
        pl.BlockSpec((D, D), const, pipeline_mode=resident),
        pl.BlockSpec((1, D), const), pl.BlockSpec((1, D), const),
    ]
    return pl.pallas_call(
        functools.partial(_mix_mem_kernel, alpha=alpha, n_in=n_in, scale=(D // MEM_HEADS) ** -0.5),
        grid=(batch, nt),
        in_specs=in_specs,
        out_specs=pl.BlockSpec((tm, D), row),
        out_shape=jax.ShapeDtypeStruct((M, D), _F32),
        compiler_params=_params("parallel", "parallel"),
        name="mix_mem",
    )(*acts, *ws, x, g1, b1, wq, kv, kv, wo, g2, b2)


_AB_CHUNK = 256
_AB_LAYOUT = (("qa", 512), ("ka", 256), ("va", 256), ("qb", 512), ("kb", 256), ("vb", 256))


def _rope(y, cos, sin_lo, sin_hi):
    return y * cos + pltpu.roll(y, 96, axis=1) * sin_lo + pltpu.roll(y, 32, axis=1) * sin_hi


def _head_mean_square(y, bd):
    sq = y * y
    hi = sq.astype(_BF16)
    lo = (sq - hi.astype(_F32)).astype(_BF16)
    return _dot(hi, bd) + _dot(lo, bd)


def _ab_proj_kernel(x_ref, w_ref, bd_ref, qg_ref, kg_ref, c2_ref, sl2_ref, sh2_ref, c1_ref, sl1_ref, sh1_ref,
                    o_ref, *, scale):
    xb = x_ref[...].astype(_BF16)
    bd = bd_ref[...]
    col = 0
    for name, width in _AB_LAYOUT:
        for c in range(col, col + width, _AB_CHUNK):
            y2 = _dot(xb, w_ref[:, c:c + _AB_CHUNK])
            for h in range(0, _AB_CHUNK, LANES):
                y = y2[:, h:h + LANES]
                if name in ("qa", "ka"):
                    gain = qg_ref[...] if name == "qa" else kg_ref[...]
                    y = y * lax.rsqrt(_head_mean_square(y, bd) + RMS_EPS) * gain
                    y = _rope(y, c2_ref[...], sl2_ref[...], sh2_ref[...])
                elif name in ("qb", "kb"):
                    y = _rope(y, c1_ref[...], sl1_ref[...], sh1_ref[...])
                if name == "qa":
                    y = y * (scale * _LOG2E)
                elif name == "qb":
                    y = y * scale
                o_ref[:, c + h:c + h + LANES] = y.astype(o_ref.dtype)
        col += width


def _ab_proj(x, w_ext, bd, qg, kg, tabs2, tabs1, *, seq, tm):
    M, D = x.shape
    N = w_ext.shape[1]
    nt = seq // tm
    tab_spec = pl.BlockSpec((tm, LANES), lambda i: (i % nt, 0))
    const = lambda shape: pl.BlockSpec(shape, lambda i: (0, 0))
    return pl.pallas_call(
        functools.partial(_ab_proj_kernel, scale=HEAD_DIM ** -0.5),
        grid=(M // tm,),
        in_specs=[pl.BlockSpec((tm, D), lambda i: (i, 0)), const((D, N)), const((LANES, LANES)),
                  const((1, LANES)), const((1, LANES))] + [tab_spec] * 6,
        out_specs=pl.BlockSpec((tm, N), lambda i: (i, 0)),
        out_shape=jax.ShapeDtypeStruct((M, N), _BF16),
        compiler_params=_params("parallel"),
        name="ab_proj",
    )(x, w_ext, bd, qg, kg, *tabs2, *tabs1)


def _stack_heads(q):
    t = q.shape[0]
    lo = lax.broadcasted_iota(jnp.int32, (t, LANES), 1) < HEAD_DIM
    zero = jnp.zeros((t, LANES), q.dtype)
    parts = []
    for c in range(0, q.shape[1], LANES):
        pair = q[:, c:c + LANES]
        parts += [jnp.where(lo, pair, zero), jnp.where(lo, zero, pair)]
    return jnp.concatenate(parts, axis=0)


def _unstack_heads(o, t):
    lo = lax.broadcasted_iota(jnp.int32, (t, LANES), 1) < HEAD_DIM
    n = o.shape[0] // t
    return jnp.concatenate(
        [jnp.where(lo, o[(2 * i) * t:(2 * i + 1) * t], o[(2 * i + 1) * t:(2 * i + 2) * t]) for i in range(n // 2)],
        axis=1)


def _attn_a_kernel(q_ref, k_ref, v_ref, o_ref, *, kc):
    tq = q_ref.shape[0]
    qs = _stack_heads(q_ref[...])
    m = l = acc = None
    for c in range(0, k_ref.shape[0], kc):
        s = _dot_nt(qs, k_ref[c:c + kc, :])
        mc = jnp.max(s, axis=-1, keepdims=True)
        m_new = mc if m is None else jnp.maximum(m, mc)
        p = jnp.exp2(s - m_new)
        lc = p[:, :LANES]
        for i in range(LANES, kc, LANES):
            lc = lc + p[:, i:i + LANES]
        pv = _dot(p.astype(_BF16), v_ref[c:c + kc, :])
        if m is None:
            l, acc = lc, pv
        else:
            corr = jnp.exp2(m - m_new)
            l, acc = l * corr + lc, acc * corr + pv
        m = m_new
    o_ref[...] = _unstack_heads(acc / jnp.sum(l, axis=-1, keepdims=True), tq).astype(o_ref.dtype)


def _attn_a(qkv, *, batch, seq, tq, kc):
    M = qkv.shape[0]
    nt = seq // tq
    gw = (A_HEADS // A_KV_HEADS) * HEAD_DIM
    k_col = A_HEADS * HEAD_DIM // LANES
    v_col = k_col + A_KV_HEADS
    return pl.pallas_call(
        functools.partial(_attn_a_kernel, kc=kc),
        grid=(batch, A_KV_HEADS, nt),
        in_specs=[
            pl.BlockSpec((tq, gw), lambda b, j, t: (b * nt + t, j)),
            pl.BlockSpec((seq, LANES), lambda b, j, t: (b, k_col + j)),
            pl.BlockSpec((seq, LANES), lambda b, j, t: (b, v_col + j)),
        ],
        out_specs=pl.BlockSpec((tq, gw), lambda b, j, t: (b * nt + t, j)),
        out_shape=jax.ShapeDtypeStruct((M, A_HEADS * HEAD_DIM), _BF16),
        compiler_params=_params("parallel", "parallel", "parallel"),
        name="attn_a",
    )(qkv, qkv, qkv)


def _b_band_masks():
    g = B_HEADS // B_KV_HEADS
    qoff = (np.arange(g * Q_BLOCK) % Q_BLOCK)[:, None]
    koff = np.arange(Q_BLOCK + 2 * WINDOW)[None, :]
    return np.stack([np.where(np.abs(koff - d - qoff) <= WINDOW, 0.0, -np.inf) for d in (0, WINDOW, 2 * WINDOW)]
                    ).astype(np.float32)


def _attn_b_kernel(sink_ref, mask_ref, q_ref, k_ref, v_ref, o_ref, *, seq):
    j = pl.program_id(1)
    t = pl.program_id(2)
    qb = Q_BLOCK
    g = B_HEADS // B_KV_HEADS
    band = qb + 2 * WINDOW
    nsub = q_ref.shape[0] // qb
    hrow = lax.broadcasted_iota(jnp.int32, (g * qb, 1), 0) // qb
    sink = jnp.zeros((g * qb, 1), _F32)
    for h in range(g):
        sink = jnp.where(hrow == h, sink_ref[j * g + h], sink)
    for u in range(nsub):
        blk = t * nsub + u
        q0 = blk * qb
        start = pl.multiple_of(jnp.clip(q0 - WINDOW, 0, seq - band), LANES)
        kind = jnp.where(blk == 0, 0, jnp.where(blk == seq // qb - 1, 2, 1))
        kb = k_ref[pl.ds(start, band), :]
        vb = v_ref[pl.ds(start, band), :]
        qs = _stack_heads(q_ref[u * qb:(u + 1) * qb, :])
        s = _dot_nt(qs, kb) + mask_ref[kind]
        m = jnp.maximum(jnp.max(s, axis=-1, keepdims=True), sink)
        p = jnp.exp(s - m)
        l = jnp.sum(p, axis=-1, keepdims=True) + jnp.exp(sink - m)
        o = _dot(p.astype(_BF16), vb) / l
        o_ref[u * qb:(u + 1) * qb, :] = _unstack_heads(o, qb).astype(o_ref.dtype)


def _attn_b(qkv, sink, *, batch, seq, tq):
    M = qkv.shape[0]
    nt = seq // tq
    assert tq % Q_BLOCK == 0 and seq // Q_BLOCK >= 2
    masks = jnp.asarray(_b_band_masks())
    gw = (B_HEADS // B_KV_HEADS) * HEAD_DIM
    base = (A_HEADS + 4 * A_KV_HEADS) * HEAD_DIM
    q_col = base // gw
    k_col = (base + B_HEADS * HEAD_DIM) // LANES
    v_col = k_col + B_KV_HEADS
    return pl.pallas_call(
        functools.partial(_attn_b_kernel, seq=seq),
        grid=(batch, B_KV_HEADS, nt),
        in_specs=[
            pl.BlockSpec(memory_space=pltpu.SMEM),
            pl.BlockSpec(masks.shape, lambda b, j, t: (0, 0, 0), pipeline_mode=pl.Buffered(1)),
            pl.BlockSpec((tq, gw), lambda b, j, t: (b * nt + t, q_col + j)),
            pl.BlockSpec((seq, LANES), lambda b, j, t: (b, k_col + j)),
            pl.BlockSpec((seq, LANES), lambda b, j, t: (b, v_col + j)),
        ],
        out_specs=pl.BlockSpec((tq, gw), lambda b, j, t: (b * nt + t, j)),
        out_shape=jax.ShapeDtypeStruct((M, B_HEADS * HEAD_DIM), _BF16),
        compiler_params=_params("parallel", "parallel", "parallel"),
        name="attn_b",
    )(sink, masks, qkv, qkv, qkv)


C_GROUP = 4
C_BAND = C_GROUP + NA_KH
_C_GROUP_UNROLL = 8


def _c_band_start(g, rows):
    return jnp.clip(g * C_GROUP - NA_KH // 2, 0, rows - C_BAND)


def _attn_c_kernel(q_ref, k_ref, v_ref, bias_ref, o_ref, *, rows):
    nq = C_GROUP * GRID_W
    nk = C_BAND * GRID_W
    groups = rows // C_GROUP

    def body(g, carry):
        q0 = pl.multiple_of(g * nq, nq)
        k0 = pl.multiple_of(_c_band_start(g, rows) * GRID_W, GRID_W)
        kind = jnp.where(g == 0, 0, jnp.where(g == groups - 1, 2, 1))
        qs = _stack_heads(q_ref[pl.ds(q0, nq), :])
        s = _dot_nt(qs, k_ref[pl.ds(k0, nk), :])
        s = s + bias_ref[kind]
        m = jnp.max(s, axis=-1, keepdims=True)
        p = jnp.exp(s - m)
        l = jnp.sum(p, axis=-1, keepdims=True)
        o = _dot(p.astype(_BF16), v_ref[pl.ds(k0, nk), :]) / l
        o_ref[pl.ds(q0, nq), :] = _unstack_heads(o, nq).astype(o_ref.dtype)
        return carry

    lax.fori_loop(0, groups, body, 0, unroll=_C_GROUP_UNROLL)


def _attn_c(qkv, bias, *, batch, seq):
    M = qkv.shape[0]
    width = qkv.shape[1] // 3
    npair = width // LANES
    rows = seq // GRID_W
    return pl.pallas_call(
        functools.partial(_attn_c_kernel, rows=rows),
        grid=(batch, npair),
        in_specs=[
            pl.BlockSpec((seq, LANES), lambda b, p: (b, p)),
            pl.BlockSpec((seq, LANES), lambda b, p: (b, npair + p)),
            pl.BlockSpec((seq, LANES), lambda b, p: (b, 2 * npair + p)),
            pl.BlockSpec((None,) + bias.shape[1:], lambda b, p: (p, 0, 0, 0)),
        ],
        out_specs=pl.BlockSpec((seq, LANES), lambda b, p: (b, p)),
        out_shape=jax.ShapeDtypeStruct((M, width), _BF16),
        compiler_params=_params("parallel", "parallel"),
        name="attn_c",
    )(qkv, qkv, qkv, bias)


def _c_window(kind, u):
    if kind == 0:
        return 0, NA_KH, 0
    if kind == 1:
        return u, u + NA_KH, -(NA_KH // 2)
    return C_GROUP, C_GROUP + NA_KH, C_GROUP - C_BAND


def _c_bias_kernel(rpb_ref, o_ref):
    nq = C_GROUP * GRID_W
    w = lax.broadcasted_iota(jnp.int32, (GRID_W, LANES), 0)
    lane = lax.broadcasted_iota(jnp.int32, (GRID_W, LANES), 1)
    kc = jnp.bitwise_and(lane, GRID_W - 1)
    upper = lane >= GRID_W
    cs = jnp.clip(w - NA_KW // 2, 0, GRID_W - NA_KW)
    inside = (kc >= cs) & (kc < cs + NA_KW)
    for hh in range(2):
        lo_t, hi_t = [], []
        for i in range(2 * NA_KH - 1):
            r = jnp.broadcast_to(rpb_ref[hh, i:i + 1, :], (GRID_W, LANES))
            lo_t.append(pltpu.roll(r, LANES - (NA_KW - 1), 1, stride=1, stride_axis=0))
            hi_t.append(pltpu.roll(r, GRID_W - (NA_KW - 1), 1, stride=1, stride_axis=0))
        neg = jnp.full((GRID_W, LANES), -jnp.inf, _F32)
        for kind in range(3):
            for u in range(C_GROUP):
                lo, hi, shift = _c_window(kind, u)
                for b2 in range(C_BAND // 2):
                    halves = []
                    for b, table in ((2 * b2, lo_t), (2 * b2 + 1, hi_t)):
                        halves.append(table[shift + b - u + NA_KH - 1] if lo <= b < hi else neg)
                    tile = jnp.where(inside, jnp.where(upper, halves[1], halves[0]), neg)
                    r0 = hh * nq + u * GRID_W
                    o_ref[kind, r0:r0 + GRID_W, b2 * LANES:(b2 + 1) * LANES] = tile


def _c_bias_table(rpb):
    H, nr, nc = rpb.shape
    assert nr == 2 * NA_KH - 1 and nc == 2 * NA_KW - 1 and 2 * GRID_W == LANES and H % 2 == 0 and C_BAND % 2 == 0
    rpb_pad = jnp.pad(rpb, ((0, 0), (0, 0), (0, LANES - nc)))
    shape = (3, 2 * C_GROUP * GRID_W, C_BAND * GRID_W)
    return pl.pallas_call(
        _c_bias_kernel,
        grid=(H // 2,),
        in_specs=[pl.BlockSpec((2, nr, LANES), lambda p: (p, 0, 0))],
        out_specs=pl.BlockSpec((None,) + shape, lambda p: (p, 0, 0, 0)),
        out_shape=jax.ShapeDtypeStruct((H // 2,) + shape, _F32),
        compiler_params=_params("parallel"),
        name="c_bias",
    )(rpb_pad)


def _rope_tables(ang):
    half = HEAD_DIM // 2
    cos, sin = jnp.cos(ang), jnp.sin(ang)
    zero = jnp.zeros_like(sin)
    cos_h = jnp.concatenate([cos, cos], axis=-1)
    sin_lo = jnp.concatenate([-sin, zero], axis=-1)
    sin_hi = jnp.concatenate([zero, sin], axis=-1)
    assert cos_h.shape[-1] == 2 * half
    return tuple(jnp.concatenate([t, t], axis=-1).astype(_F32) for t in (cos_h, sin_lo, sin_hi))


def _rope_angles(pos, dim):
    inv = ROPE_THETA ** (-jnp.arange(0, dim, 2, dtype=_F32) / dim)
    return pos[:, None] * inv[None, :]


def _ab_columns():
    d = HEAD_DIM
    cols, c = [], 0
    for n_q, n_kv in ((A_HEADS, A_KV_HEADS), (B_HEADS, B_KV_HEADS)):
        cols += list(range(c, c + n_q * d))
        c += n_q * d
        for _ in range(2):
            for h in range(n_kv):
                cols += 2 * list(range(c + h * d, c + (h + 1) * d))
            c += n_kv * d
    return np.asarray(cols, dtype=np.int32)


def _pick(n, pref):
    t = min(n, pref)
    assert n % t == 0, (n, pref)
    return t


def kernel(x, mem, ln_g, ln_b, ffn_w_gate, ffn_w_up, ffn_w_down, ab_w_in, ab_w_out, ab_q_gain, ab_k_gain, ab_sink,
           c_w_in, c_w_out, c_rpb, mem_w_q, mem_w_kv, mem_w_o):
    B, S, D = x.shape
    depth = ln_g.shape[0]
    M = B * S
    MT = mem.shape[1]
    rows = S // GRID_W
    assert S % GRID_W == 0 and rows >= C_BAND and rows % C_GROUP == 0 and NA_KH // 2 == C_GROUP
    assert S % Q_BLOCK == 0 and S >= Q_BLOCK + 2 * WINDOW
    assert D % (MEM_HEADS * LANES) == 0 and D == 2 * A_HEADS * HEAD_DIM
    alpha = (2.0 * depth) ** 0.25

    t = jnp.arange(S)
    row = (t // GRID_W).astype(_F32)
    colp = (t % GRID_W).astype(_F32)
    ang_2d = jnp.concatenate([_rope_angles(row, HEAD_DIM // 2), _rope_angles(colp, HEAD_DIM // 2)], axis=-1)
    ang_1d = _rope_angles(t.astype(_F32), HEAD_DIM)
    tabs2 = _rope_tables(ang_2d)
    tabs1 = _rope_tables(ang_1d)
    lane_head = np.arange(LANES) // HEAD_DIM
    bd = jnp.asarray((lane_head[:, None] == lane_head[None, :]) / HEAD_DIM, dtype=_BF16)

    wg = ffn_w_gate.astype(_BF16)
    wu = ffn_w_up.astype(_BF16)
    wd = ffn_w_down.astype(_BF16)
    ab_cols = _ab_columns()

    tm_ffn = _pick(M, 512)
    tf = _pick(wg.shape[-1], 256)
    tm = _pick(M, 512)
    tm_ab = _pick(S, 512)

    xf = x.reshape(M, D)
    memf = mem.reshape(B * MT, D)
    for i in range(depth):
        j = i // 2
        g = lambda k: ln_g[i, k].reshape(1, D)
        bb = lambda k: ln_b[i, k].reshape(1, D)
        xf = _ffn_ln(xf, wg, wu, wd, g(0), bb(0), i, 0, alpha=alpha, tm=tm_ffn, tf=tf)
        if i % 2 == 0:
            w_ext = ab_w_in[j][:, ab_cols].astype(_BF16)
            qg = jnp.tile(ab_q_gain[j], LANES // HEAD_DIM).reshape(1, LANES)
            kg = jnp.tile(ab_k_gain[j], LANES // HEAD_DIM).reshape(1, LANES)
            qkv = _ab_proj(xf, w_ext, bd, qg, kg, tabs2, tabs1, seq=S, tm=tm_ab)
            out_a = _attn_a(qkv, batch=B, seq=S, tq=_pick(S, 256), kc=_pick(S, 512))
            out_b = _attn_b(qkv, ab_sink[j], batch=B, seq=S, tq=_pick(S, 4 * Q_BLOCK))
            w_out = ab_w_out[j].astype(_BF16)
            na = A_HEADS * HEAD_DIM
            acts, ws = [out_a, out_b], [w_out[:na], w_out[na:]]
        else:
            width = c_w_in.shape[-1] // 3
            scale_cols = jnp.where(jnp.arange(3 * width) < width, HEAD_DIM ** -0.5, 1.0).astype(_F32)
            qkv = _proj(xf, (c_w_in[j] * scale_cols[None, :]).astype(_BF16), tm=tm, tn=_pick(3 * width, 512))
            acts, ws = [_attn_c(qkv, _c_bias_table(c_rpb[j]), batch=B, seq=S)], [c_w_out[j].astype(_BF16)]
        kv = _proj(memf, mem_w_kv[i].astype(_BF16), tm=_pick(B * MT, 512), tn=_pick(2 * D, 512))
        xf = _mix_mem(acts, ws, xf, g(1), bb(1), mem_w_q[i].astype(_BF16), kv, mem_w_o[i].astype(_BF16), g(2), bb(2),
                      alpha=alpha, batch=B, seq=S, tm=tm_ab)
        xf = _ffn_ln(xf, wg, wu, wd, g(3), bb(3), i, 1, alpha=alpha, tm=tm_ffn, tf=tf)
    return xf.reshape(B, S, D)
```

```python
import functools

import numpy as np
import jax
import jax.numpy as jnp
from jax import lax
from jax.experimental import pallas as pl
from jax.experimental.pallas import tpu as pltpu

GRID_W = 64
HEAD_DIM = 64
A_HEADS = 8
A_KV_HEADS = 2
B_HEADS = 8
B_KV_HEADS = 2
Q_BLOCK = 128
WINDOW = 128
NA_KH = 8
NA_KW = 16
MEM_HEADS = 4
ROPE_THETA = 10000.0
LN_EPS = 1e-5
RMS_EPS = 1e-6

LANES = 128
VMEM_LIMIT_BYTES = 48 * 1024 * 1024

_BF16 = jnp.bfloat16
_F32 = jnp.float32
_NT = (((1,), (1,)), ((), ()))
_LOG2E = 1.4426950408889634


def _dot(a, b):
    return jnp.dot(a, b, preferred_element_type=_F32)


def _dot_nt(a, b):
    return lax.dot_general(a, b, _NT, preferred_element_type=_F32)


def _params(*sem):
    return pltpu.CompilerParams(dimension_semantics=sem, vmem_limit_bytes=VMEM_LIMIT_BYTES)


def _layer_norm(z, g, b):
    mu = jnp.mean(z, axis=-1, keepdims=True)
    zc = z - mu
    var = jnp.mean(zc * zc, axis=-1, keepdims=True)
    return zc * lax.rsqrt(var + LN_EPS) * g + b


_FFN_OUT_SLABS = 2


def _ffn_ln_kernel(x_ref, wg_ref, wu_ref, wd_ref, g_ref, b_ref, o_ref, act_ref, *, alpha, tf):
    xb = x_ref[...].astype(_BF16)
    for c in range(0, wg_ref.shape[1], tf):
        gate = _dot(xb, wg_ref[:, c:c + tf])
        up = _dot(xb, wu_ref[:, c:c + tf])
        act_ref[:, c:c + tf] = (gate * jax.nn.sigmoid(gate) * up).astype(_BF16)
    slab = x_ref.shape[0] // _FFN_OUT_SLABS
    for r in range(0, x_ref.shape[0], slab):
        y = _dot(act_ref[r:r + slab, :], wd_ref[...])
        o_ref[r:r + slab, :] = _layer_norm(alpha * x_ref[r:r + slab, :] + 0.5 * y, g_ref[...], b_ref[...])


def _ffn_ln(x, wg, wu, wd, g, b, layer, half, *, alpha, tm, tf):
    M, D = x.shape
    F = wg.shape[-1]
    resident = pl.Buffered(1)
    return pl.pallas_call(
        functools.partial(_ffn_ln_kernel, alpha=alpha, tf=tf),
        grid=(M // tm,),
        in_specs=[
            pl.BlockSpec((tm, D), lambda i: (i, 0)),
            pl.BlockSpec((None, None, D, F), lambda i: (layer, half, 0, 0), pipeline_mode=resident),
            pl.BlockSpec((None, None, D, F), lambda i: (layer, half, 0, 0), pipeline_mode=resident),
            pl.BlockSpec((None, None, F, D), lambda i: (layer, half, 0, 0), pipeline_mode=resident),
            pl.BlockSpec((1, D), lambda i: (0, 0)),
            pl.BlockSpec((1, D), lambda i: (0, 0)),
        ],
        out_specs=pl.BlockSpec((tm, D), lambda i: (i, 0)),
        out_shape=jax.ShapeDtypeStruct((M, D), _F32),
        scratch_shapes=[pltpu.VMEM((tm, F), _BF16)],
        compiler_params=_params("parallel"),
        name="ffn_ln",
    )(x, wg, wu, wd, g, b)


def _proj_kernel(x_ref, w_ref, o_ref, *, tn, scaled, scale):
    xb = x_ref[...].astype(_BF16)
    for c in range(0, w_ref.shape[1], tn):
        y = _dot(xb, w_ref[:, c:c + tn])
        if c < scaled:
            y = y * scale
        o_ref[:, c:c + tn] = y.astype(o_ref.dtype)


def _proj(x, w, *, tm, tn, scaled=0, scale=1.0):
    M, K = x.shape
    N = w.shape[1]
    assert scaled % tn == 0
    return pl.pallas_call(
        functools.partial(_proj_kernel, tn=tn, scaled=scaled, scale=scale),
        grid=(M // tm,),
        in_specs=[pl.BlockSpec((tm, K), lambda i: (i, 0)),
                  pl.BlockSpec((K, N), lambda i: (0, 0), pipeline_mode=pl.Buffered(1))],
        out_specs=pl.BlockSpec((tm, N), lambda i: (i, 0)),
        out_shape=jax.ShapeDtypeStruct((M, N), _BF16),
        compiler_params=_params("parallel"),
        name="proj",
    )(x, w)


def _softmax2_pv(s, v):
    m = jnp.max(s, axis=-1, keepdims=True)
    p = jnp.exp2(s - m)
    l = jnp.sum(p, axis=-1, keepdims=True)
    return _dot(p.astype(_BF16), v) / l


_MIX_SLABS = 1


def _mix_mem_kernel(*refs, alpha, n_in, scale):
    a_refs = refs[:n_in]
    w_refs = refs[n_in:2 * n_in]
    x_ref, g1_ref, b1_ref, wq_ref, k_ref, v_ref, wo_ref, g2_ref, b2_ref, o_ref = refs[2 * n_in:]
    hd = x_ref.shape[1] // MEM_HEADS
    slab = x_ref.shape[0] // _MIX_SLABS
    for r in range(0, x_ref.shape[0], slab):
        rows = slice(r, r + slab)
        y = _dot(a_refs[0][rows, :], w_refs[0][...])
        for a_ref, w_ref in zip(a_refs[1:], w_refs[1:]):
            y = y + _dot(a_ref[rows, :], w_ref[...])
        x1 = _layer_norm(alpha * x_ref[rows, :] + y, g1_ref[...], b1_ref[...])
        q = (_dot(x1.astype(_BF16), wq_ref[...]) * scale).astype(_BF16)
        heads = []
        for h in range(MEM_HEADS):
            sl = slice(h * hd, (h + 1) * hd)
            heads.append(_softmax2_pv(_dot_nt(q[:, sl], k_ref[:, sl]), v_ref[:, sl]).astype(_BF16))
        y2 = _dot(jnp.concatenate(heads, axis=1), wo_ref[...])
        o_ref[rows, :] = _layer_norm(alpha * x1 + y2, g2_ref[...], b2_ref[...])


def _mix_mem(acts, ws, x, g1, b1, wq, kv, wo, g2, b2, *, alpha, batch, seq, tm):
    M, D = x.shape
    n_in = len(acts)
    mt = kv.shape[0] // batch
    nt = seq // tm
    row = lambda b, t: (b * nt + t, 0)
    const = lambda b, t: (0, 0)
    resident = pl.Buffered(1)
    in_specs = [pl.BlockSpec((tm, a.shape[1]), row) for a in acts]
    in_specs += [pl.BlockSpec(w.shape, const, pipeline_mode=resident) for w in ws]
    in_specs += [
        pl.BlockSpec((tm, D), row), pl.BlockSpec((1, D), const), pl.BlockSpec((1, D), const),
        pl.BlockSpec((D, D), const, pipeline_mode=resident),
        pl.BlockSpec((mt, D), lambda b, t: (b, 0)),
        pl.BlockSpec((mt, D), lambda b, t: (b, 1)),
        pl.BlockSpec((D, D), const, pipeline_mode=resident),
        pl.BlockSpec((1, D), const), pl.BlockSpec((1, D), const),
    ]
    return pl.pallas_call(
        functools.partial(_mix_mem_kernel, alpha=alpha, n_in=n_in, scale=(D // MEM_HEADS) ** -0.5 * _LOG2E),
        grid=(batch, nt),
        in_specs=in_specs,
        out_specs=pl.BlockSpec((tm, D), row),
        out_shape=jax.ShapeDtypeStruct((M, D), _F32),
        compiler_params=_params("parallel", "parallel"),
        name="mix_mem",
    )(*acts, *ws, x, g1, b1, wq, kv, kv, wo, g2, b2)


_AB_CHUNK = 256
_AB_LAYOUT = (("qa", 512), ("ka", 256), ("va", 256), ("qb", 512), ("kb", 256), ("vb", 256))


def _rope(y, cos, sin_lo, sin_hi):
    return y * cos + pltpu.roll(y, 96, axis=1) * sin_lo + pltpu.roll(y, 32, axis=1) * sin_hi


def _head_mean_square(y, bd):
    sq = y * y
    hi = sq.astype(_BF16)
    lo = (sq - hi.astype(_F32)).astype(_BF16)
    return _dot(hi, bd) + _dot(lo, bd)


def _ab_proj_kernel(x_ref, w_ref, bd_ref, qg_ref, kg_ref, c2_ref, sl2_ref, sh2_ref, c1_ref, sl1_ref, sh1_ref,
                    o_ref, *, scale):
    xb = x_ref[...].astype(_BF16)
    bd = bd_ref[...]
    col = 0
    for name, width in _AB_LAYOUT:
        for c in range(col, col + width, _AB_CHUNK):
            y2 = _dot(xb, w_ref[:, c:c + _AB_CHUNK])
            for h in range(0, _AB_CHUNK, LANES):
                y = y2[:, h:h + LANES]
                if name in ("qa", "ka"):
                    gain = qg_ref[...] if name == "qa" else kg_ref[...]
                    y = y * lax.rsqrt(_head_mean_square(y, bd) + RMS_EPS) * gain
                    y = _rope(y, c2_ref[...], sl2_ref[...], sh2_ref[...])
                elif name in ("qb", "kb"):
                    y = _rope(y, c1_ref[...], sl1_ref[...], sh1_ref[...])
                if name in ("qa", "qb"):
                    y = y * (scale * _LOG2E)
                o_ref[:, c + h:c + h + LANES] = y.astype(o_ref.dtype)
        col += width


def _ab_proj(x, w_ext, bd, qg, kg, tabs2, tabs1, *, seq, tm):
    M, D = x.shape
    N = w_ext.shape[1]
    nt = seq // tm
    tab_spec = pl.BlockSpec((tm, LANES), lambda i: (i % nt, 0))
    const = lambda shape: pl.BlockSpec(shape, lambda i: (0, 0))
    return pl.pallas_call(
        functools.partial(_ab_proj_kernel, scale=HEAD_DIM ** -0.5),
        grid=(M // tm,),
        in_specs=[pl.BlockSpec((tm, D), lambda i: (i, 0)), const((D, N)), const((LANES, LANES)),
                  const((1, LANES)), const((1, LANES))] + [tab_spec] * 6,
        out_specs=pl.BlockSpec((tm, N), lambda i: (i, 0)),
        out_shape=jax.ShapeDtypeStruct((M, N), _BF16),
        compiler_params=_params("parallel"),
        name="ab_proj",
    )(x, w_ext, bd, qg, kg, *tabs2, *tabs1)


def _stack_heads(q):
    t = q.shape[0]
    lo = lax.broadcasted_iota(jnp.int32, (t, LANES), 1) < HEAD_DIM
    zero = jnp.zeros((t, LANES), q.dtype)
    parts = []
    for c in range(0, q.shape[1], LANES):
        pair = q[:, c:c + LANES]
        parts += [jnp.where(lo, pair, zero), jnp.where(lo, zero, pair)]
    return jnp.concatenate(parts, axis=0)


def _unstack_heads(o, t):
    lo = lax.broadcasted_iota(jnp.int32, (t, LANES), 1) < HEAD_DIM
    n = o.shape[0] // t
    return jnp.concatenate(
        [jnp.where(lo, o[(2 * i) * t:(2 * i + 1) * t], o[(2 * i + 1) * t:(2 * i + 2) * t]) for i in range(n // 2)],
        axis=1)


def _attn_a_kernel(q_ref, k_ref, v_ref, o_ref, *, kc):
    tq = q_ref.shape[0]
    qs = _stack_heads(q_ref[...])
    m = l = acc = None
    for c in range(0, k_ref.shape[0], kc):
        s = _dot_nt(qs, k_ref[c:c + kc, :])
        mc = jnp.max(s, axis=-1, keepdims=True)
        m_new = mc if m is None else jnp.maximum(m, mc)
        p = jnp.exp2(s - m_new)
        lc = p[:, :LANES]
        for i in range(LANES, kc, LANES):
            lc = lc + p[:, i:i + LANES]
        pv = _dot(p.astype(_BF16), v_ref[c:c + kc, :])
        if m is None:
            l, acc = lc, pv
        else:
            corr = jnp.exp2(m - m_new)
            l, acc = l * corr + lc, acc * corr + pv
        m = m_new
    o_ref[...] = _unstack_heads(acc / jnp.sum(l, axis=-1, keepdims=True), tq).astype(o_ref.dtype)


def _attn_a_bounded_kernel(q_ref, k_ref, v_ref, o_ref, *, kc):
    tq = q_ref.shape[0]
    qs = _stack_heads(q_ref[...])
    l = acc = None
    for c in range(0, k_ref.shape[0], kc):
        p = jnp.exp2(_dot_nt(qs, k_ref[c:c + kc, :]))
        lc = p[:, :LANES]
        for i in range(LANES, kc, LANES):
            lc = lc + p[:, i:i + LANES]
        pv = _dot(p.astype(_BF16), v_ref[c:c + kc, :])
        l, acc = (lc, pv) if l is None else (l + lc, acc + pv)
    o_ref[...] = _unstack_heads(acc / jnp.sum(l, axis=-1, keepdims=True), tq).astype(o_ref.dtype)


A_SCORE_BOUND = 60.0


def _a_score_bound(q_gain, k_gain):
    return 1.02 * HEAD_DIM ** 0.5 * _LOG2E * jnp.max(jnp.abs(q_gain)) * jnp.max(jnp.abs(k_gain))


def _attn_a(qkv, *, batch, seq, tq, kc, bounded):
    M = qkv.shape[0]
    nt = seq // tq
    gw = (A_HEADS // A_KV_HEADS) * HEAD_DIM
    k_col = A_HEADS * HEAD_DIM // LANES
    v_col = k_col + A_KV_HEADS
    return pl.pallas_call(
        functools.partial(_attn_a_bounded_kernel if bounded else _attn_a_kernel, kc=kc),
        grid=(batch, A_KV_HEADS, nt),
        in_specs=[
            pl.BlockSpec((tq, gw), lambda b, j, t: (b * nt + t, j)),
            pl.BlockSpec((seq, LANES), lambda b, j, t: (b, k_col + j)),
            pl.BlockSpec((seq, LANES), lambda b, j, t: (b, v_col + j)),
        ],
        out_specs=pl.BlockSpec((tq, gw), lambda b, j, t: (b * nt + t, j)),
        out_shape=jax.ShapeDtypeStruct((M, A_HEADS * HEAD_DIM), _BF16),
        compiler_params=_params("parallel", "parallel", "parallel"),
        name="attn_a",
    )(qkv, qkv, qkv)


def _b_band_masks():
    g = B_HEADS // B_KV_HEADS
    qoff = (np.arange(g * Q_BLOCK) % Q_BLOCK)[:, None]
    koff = np.arange(Q_BLOCK + 2 * WINDOW)[None, :]
    return np.stack([np.where(np.abs(koff - d - qoff) <= WINDOW, 0.0, -np.inf) for d in (0, WINDOW, 2 * WINDOW)]
                    ).astype(np.float32)


def _attn_b_kernel(sink_ref, mask_ref, q_ref, k_ref, v_ref, o_ref, *, seq):
    j = pl.program_id(1)
    t = pl.program_id(2)
    qb = Q_BLOCK
    g = B_HEADS // B_KV_HEADS
    band = qb + 2 * WINDOW
    nsub = q_ref.shape[0] // qb
    hrow = lax.broadcasted_iota(jnp.int32, (g * qb, 1), 0) // qb
    sink = jnp.zeros((g * qb, 1), _F32)
    for h in range(g):
        sink = jnp.where(hrow == h, sink_ref[j * g + h] * _LOG2E, sink)
    for u in range(nsub):
        blk = t * nsub + u
        q0 = blk * qb
        start = pl.multiple_of(jnp.clip(q0 - WINDOW, 0, seq - band), LANES)
        kind = jnp.where(blk == 0, 0, jnp.where(blk == seq // qb - 1, 2, 1))
        kb = k_ref[pl.ds(start, band), :]
        vb = v_ref[pl.ds(start, band), :]
        qs = _stack_heads(q_ref[u * qb:(u + 1) * qb, :])
        s = _dot_nt(qs, kb) + mask_ref[kind]
        m = jnp.maximum(jnp.max(s, axis=-1, keepdims=True), sink)
        p = jnp.exp2(s - m)
        l = jnp.sum(p, axis=-1, keepdims=True) + jnp.exp2(sink - m)
        o = _dot(p.astype(_BF16), vb) / l
        o_ref[u * qb:(u + 1) * qb, :] = _unstack_heads(o, qb).astype(o_ref.dtype)


def _attn_b(qkv, sink, *, batch, seq, tq):
    M = qkv.shape[0]
    nt = seq // tq
    assert tq % Q_BLOCK == 0 and seq // Q_BLOCK >= 2
    masks = jnp.asarray(_b_band_masks())
    gw = (B_HEADS // B_KV_HEADS) * HEAD_DIM
    base = (A_HEADS + 4 * A_KV_HEADS) * HEAD_DIM
    q_col = base // gw
    k_col = (base + B_HEADS * HEAD_DIM) // LANES
    v_col = k_col + B_KV_HEADS
    return pl.pallas_call(
        functools.partial(_attn_b_kernel, seq=seq),
        grid=(batch, B_KV_HEADS, nt),
        in_specs=[
            pl.BlockSpec(memory_space=pltpu.SMEM),
            pl.BlockSpec(masks.shape, lambda b, j, t: (0, 0, 0), pipeline_mode=pl.Buffered(1)),
            pl.BlockSpec((tq, gw), lambda b, j, t: (b * nt + t, q_col + j)),
            pl.BlockSpec((seq, LANES), lambda b, j, t: (b, k_col + j)),
            pl.BlockSpec((seq, LANES), lambda b, j, t: (b, v_col + j)),
        ],
        out_specs=pl.BlockSpec((tq, gw), lambda b, j, t: (b * nt + t, j)),
        out_shape=jax.ShapeDtypeStruct((M, B_HEADS * HEAD_DIM), _BF16),
        compiler_params=_params("parallel", "parallel", "parallel"),
        name="attn_b",
    )(sink, masks, qkv, qkv, qkv)


C_GROUP = 4
C_BAND = C_GROUP + NA_KH
_C_GROUP_UNROLL = 16


def _c_band_start(g, rows):
    return jnp.clip(g * C_GROUP - NA_KH // 2, 0, rows - C_BAND)


def _attn_c_kernel(q_ref, k_ref, v_ref, bias_ref, o_ref, *, rows):
    nq = C_GROUP * GRID_W
    nk = C_BAND * GRID_W
    groups = rows // C_GROUP

    def body(g, carry):
        q0 = pl.multiple_of(g * nq, nq)
        k0 = pl.multiple_of(_c_band_start(g, rows) * GRID_W, GRID_W)
        kind = jnp.where(g == 0, 0, jnp.where(g == groups - 1, 2, 1))
        qs = _stack_heads(q_ref[pl.ds(q0, nq), :])
        s = _dot_nt(qs, k_ref[pl.ds(k0, nk), :])
        o = _softmax2_pv(s + bias_ref[kind], v_ref[pl.ds(k0, nk), :])
        o_ref[pl.ds(q0, nq), :] = _unstack_heads(o, nq).astype(o_ref.dtype)
        return carry

    lax.fori_loop(0, groups, body, 0, unroll=_C_GROUP_UNROLL)


def _attn_c(qkv, bias, *, batch, seq):
    M = qkv.shape[0]
    width = qkv.shape[1] // 3
    npair = width // LANES
    rows = seq // GRID_W
    return pl.pallas_call(
        functools.partial(_attn_c_kernel, rows=rows),
        grid=(batch, npair),
        in_specs=[
            pl.BlockSpec((seq, LANES), lambda b, p: (b, p)),
            pl.BlockSpec((seq, LANES), lambda b, p: (b, npair + p)),
            pl.BlockSpec((seq, LANES), lambda b, p: (b, 2 * npair + p)),
            pl.BlockSpec((None,) + bias.shape[1:], lambda b, p: (p, 0, 0, 0)),
        ],
        out_specs=pl.BlockSpec((seq, LANES), lambda b, p: (b, p)),
        out_shape=jax.ShapeDtypeStruct((M, width), _BF16),
        compiler_params=_params("parallel", "parallel"),
        name="attn_c",
    )(qkv, qkv, qkv, bias)


def _c_window(kind, u):
    if kind == 0:
        return 0, NA_KH, 0
    if kind == 1:
        return u, u + NA_KH, -(NA_KH // 2)
    return C_GROUP, C_GROUP + NA_KH, C_GROUP - C_BAND


def _c_bias_kernel(rpb_ref, o_ref):
    nq = C_GROUP * GRID_W
    w = lax.broadcasted_iota(jnp.int32, (GRID_W, LANES), 0)
    lane = lax.broadcasted_iota(jnp.int32, (GRID_W, LANES), 1)
    kc = jnp.bitwise_and(lane, GRID_W - 1)
    upper = lane >= GRID_W
    cs = jnp.clip(w - NA_KW // 2, 0, GRID_W - NA_KW)
    inside = (kc >= cs) & (kc < cs + NA_KW)
    for hh in range(2):
        lo_t, hi_t = [], []
        for i in range(2 * NA_KH - 1):
            r = jnp.broadcast_to(rpb_ref[hh, i:i + 1, :], (GRID_W, LANES))
            lo_t.append(pltpu.roll(r, LANES - (NA_KW - 1), 1, stride=1, stride_axis=0))
            hi_t.append(pltpu.roll(r, GRID_W - (NA_KW - 1), 1, stride=1, stride_axis=0))
        neg = jnp.full((GRID_W, LANES), -jnp.inf, _F32)
        for kind in range(3):
            for u in range(C_GROUP):
                lo, hi, shift = _c_window(kind, u)
                for b2 in range(C_BAND // 2):
                    halves = []
                    for b, table in ((2 * b2, lo_t), (2 * b2 + 1, hi_t)):
                        halves.append(table[shift + b - u + NA_KH - 1] if lo <= b < hi else neg)
                    tile = jnp.where(inside, jnp.where(upper, halves[1], halves[0]) * _LOG2E, neg)
                    r0 = hh * nq + u * GRID_W
                    o_ref[kind, r0:r0 + GRID_W, b2 * LANES:(b2 + 1) * LANES] = tile


def _c_bias_table(rpb):
    H, nr, nc = rpb.shape
    assert nr == 2 * NA_KH - 1 and nc == 2 * NA_KW - 1 and 2 * GRID_W == LANES and H % 2 == 0 and C_BAND % 2 == 0
    rpb_pad = jnp.pad(rpb, ((0, 0), (0, 0), (0, LANES - nc)))
    shape = (3, 2 * C_GROUP * GRID_W, C_BAND * GRID_W)
    return pl.pallas_call(
        _c_bias_kernel,
        grid=(H // 2,),
        in_specs=[pl.BlockSpec((2, nr, LANES), lambda p: (p, 0, 0))],
        out_specs=pl.BlockSpec((None,) + shape, lambda p: (p, 0, 0, 0)),
        out_shape=jax.ShapeDtypeStruct((H // 2,) + shape, _F32),
        compiler_params=_params("parallel"),
        name="c_bias",
    )(rpb_pad)


def _rope_tables(ang):
    half = HEAD_DIM // 2
    cos, sin = jnp.cos(ang), jnp.sin(ang)
    zero = jnp.zeros_like(sin)
    cos_h = jnp.concatenate([cos, cos], axis=-1)
    sin_lo = jnp.concatenate([-sin, zero], axis=-1)
    sin_hi = jnp.concatenate([zero, sin], axis=-1)
    assert cos_h.shape[-1] == 2 * half
    return tuple(jnp.concatenate([t, t], axis=-1).astype(_F32) for t in (cos_h, sin_lo, sin_hi))


def _rope_angles(pos, dim):
    inv = ROPE_THETA ** (-jnp.arange(0, dim, 2, dtype=_F32) / dim)
    return pos[:, None] * inv[None, :]


def _ab_columns():
    d = HEAD_DIM
    cols, c = [], 0
    for n_q, n_kv in ((A_HEADS, A_KV_HEADS), (B_HEADS, B_KV_HEADS)):
        cols += list(range(c, c + n_q * d))
        c += n_q * d
        for _ in range(2):
            for h in range(n_kv):
                cols += 2 * list(range(c + h * d, c + (h + 1) * d))
            c += n_kv * d
    return np.asarray(cols, dtype=np.int32)


def _pick(n, pref):
    t = min(n, pref)
    assert n % t == 0, (n, pref)
    return t


def kernel(x, mem, ln_g, ln_b, ffn_w_gate, ffn_w_up, ffn_w_down, ab_w_in, ab_w_out, ab_q_gain, ab_k_gain, ab_sink,
           c_w_in, c_w_out, c_rpb, mem_w_q, mem_w_kv, mem_w_o):
    B, S, D = x.shape
    depth = ln_g.shape[0]
    M = B * S
    MT = mem.shape[1]
    rows = S // GRID_W
    assert S % GRID_W == 0 and rows >= C_BAND and rows % C_GROUP == 0 and NA_KH // 2 == C_GROUP
    assert S % Q_BLOCK == 0 and S >= Q_BLOCK + 2 * WINDOW
    assert D % (MEM_HEADS * LANES) == 0 and D == 2 * A_HEADS * HEAD_DIM
    alpha = (2.0 * depth) ** 0.25

    t = jnp.arange(S)
    row = (t // GRID_W).astype(_F32)
    colp = (t % GRID_W).astype(_F32)
    ang_2d = jnp.concatenate([_rope_angles(row, HEAD_DIM // 2), _rope_angles(colp, HEAD_DIM // 2)], axis=-1)
    ang_1d = _rope_angles(t.astype(_F32), HEAD_DIM)
    tabs2 = _rope_tables(ang_2d)
    tabs1 = _rope_tables(ang_1d)
    lane_head = np.arange(LANES) // HEAD_DIM
    bd = jnp.asarray((lane_head[:, None] == lane_head[None, :]) / HEAD_DIM, dtype=_BF16)

    wg = ffn_w_gate.astype(_BF16)
    wu = ffn_w_up.astype(_BF16)
    wd = ffn_w_down.astype(_BF16)
    ab_cols = _ab_columns()

    tm_ffn = _pick(M, 512)
    tf = _pick(wg.shape[-1], 256)
    tm = _pick(M, 512)
    tm_ab = _pick(S, 512)

    xf = x.reshape(M, D)
    memf = mem.reshape(B * MT, D)
    for i in range(depth):
        j = i // 2
        g = lambda k: ln_g[i, k].reshape(1, D)
        bb = lambda k: ln_b[i, k].reshape(1, D)
        xf = _ffn_ln(xf, wg, wu, wd, g(0), bb(0), i, 0, alpha=alpha, tm=tm_ffn, tf=tf)
        if i % 2 == 0:
            w_ext = ab_w_in[j][:, ab_cols].astype(_BF16)
            qg = jnp.tile(ab_q_gain[j], LANES // HEAD_DIM).reshape(1, LANES)
            kg = jnp.tile(ab_k_gain[j], LANES // HEAD_DIM).reshape(1, LANES)
            qkv = _ab_proj(xf, w_ext, bd, qg, kg, tabs2, tabs1, seq=S, tm=tm_ab)
            attn_a = functools.partial(_attn_a, batch=B, seq=S, tq=_pick(S, 512))
            out_a = lax.cond(_a_score_bound(ab_q_gain[j], ab_k_gain[j]) <= A_SCORE_BOUND,
                             functools.partial(attn_a, kc=_pick(S, 256), bounded=True),
                             functools.partial(attn_a, kc=_pick(S, 512), bounded=False), qkv)
            out_b = _attn_b(qkv, ab_sink[j], batch=B, seq=S, tq=_pick(S, 8 * Q_BLOCK))
            w_out = ab_w_out[j].astype(_BF16)
            na = A_HEADS * HEAD_DIM
            acts, ws = [out_a, out_b], [w_out[:na], w_out[na:]]
        else:
            width = c_w_in.shape[-1] // 3
            qkv = _proj(xf, c_w_in[j].astype(_BF16), tm=tm, tn=_pick(width, 512),
                        scaled=width, scale=HEAD_DIM ** -0.5 * _LOG2E)
            acts, ws = [_attn_c(qkv, _c_bias_table(c_rpb[j]), batch=B, seq=S)], [c_w_out[j].astype(_BF16)]
        kv = _proj(memf, mem_w_kv[i].astype(_BF16), tm=_pick(B * MT, 512), tn=_pick(2 * D, 512))
        xf = _mix_mem(acts, ws, xf, g(1), bb(1), mem_w_q[i].astype(_BF16), kv, mem_w_o[i].astype(_BF16), g(2), bb(2),
                      alpha=alpha, batch=B, seq=S, tm=tm_ab)
        xf = _ffn_ln(xf, wg, wu, wd, g(3), bb(3), i, 1, alpha=alpha, tm=tm_ffn, tf=tf)
    return xf.reshape(B, S, D)
```

```python
import functools

import numpy as np
import jax
import jax.numpy as jnp
from jax import lax
from jax.experimental import pallas as pl
from jax.experimental.pallas import tpu as pltpu

GRID_W = 64
HEAD_DIM = 64
A_HEADS = 8
A_KV_HEADS = 2
B_HEADS = 8
B_KV_HEADS = 2
Q_BLOCK = 128
WINDOW = 128
NA_KH = 8
NA_KW = 16
MEM_HEADS = 4
ROPE_THETA = 10000.0
LN_EPS = 1e-5
RMS_EPS = 1e-6

LANES = 128
VMEM_LIMIT_BYTES = 48 * 1024 * 1024

_BF16 = jnp.bfloat16
_F32 = jnp.float32
_NT = (((1,), (1,)), ((), ()))
_LOG2E = 1.4426950408889634


def _dot(a, b):
    return jnp.dot(a, b, preferred_element_type=_F32)


def _dot_nt(a, b):
    return lax.dot_general(a, b, _NT, preferred_element_type=_F32)


def _params(*sem):
    return pltpu.CompilerParams(dimension_semantics=sem, vmem_limit_bytes=VMEM_LIMIT_BYTES)


def _layer_norm(z, g, b):
    mu = jnp.mean(z, axis=-1, keepdims=True)
    zc = z - mu
    var = jnp.mean(zc * zc, axis=-1, keepdims=True)
    return zc * lax.rsqrt(var + LN_EPS) * g + b


_FFN_OUT_SLABS = 2


def _ffn_ln_kernel(x_ref, wg_ref, wu_ref, wd_ref, g_ref, b_ref, o_ref, act_ref, *, alpha, tf):
    xb = x_ref[...].astype(_BF16)
    for c in range(0, wg_ref.shape[1], tf):
        gate = _dot(xb, wg_ref[:, c:c + tf])
        up = _dot(xb, wu_ref[:, c:c + tf])
        act_ref[:, c:c + tf] = (gate * jax.nn.sigmoid(gate) * up).astype(_BF16)
    slab = x_ref.shape[0] // _FFN_OUT_SLABS
    for r in range(0, x_ref.shape[0], slab):
        y = _dot(act_ref[r:r + slab, :], wd_ref[...])
        o_ref[r:r + slab, :] = _layer_norm(alpha * x_ref[r:r + slab, :] + 0.5 * y, g_ref[...], b_ref[...])


def _ffn_ln(x, wg, wu, wd, g, b, layer, half, *, alpha, tm, tf):
    M, D = x.shape
    F = wg.shape[-1]
    resident = pl.Buffered(1)
    return pl.pallas_call(
        functools.partial(_ffn_ln_kernel, alpha=alpha, tf=tf),
        grid=(M // tm,),
        in_specs=[
            pl.BlockSpec((tm, D), lambda i: (i, 0)),
            pl.BlockSpec((None, None, D, F), lambda i: (layer, half, 0, 0), pipeline_mode=resident),
            pl.BlockSpec((None, None, D, F), lambda i: (layer, half, 0, 0), pipeline_mode=resident),
            pl.BlockSpec((None, None, F, D), lambda i: (layer, half, 0, 0), pipeline_mode=resident),
            pl.BlockSpec((1, D), lambda i: (0, 0)),
            pl.BlockSpec((1, D), lambda i: (0, 0)),
        ],
        out_specs=pl.BlockSpec((tm, D), lambda i: (i, 0)),
        out_shape=jax.ShapeDtypeStruct((M, D), _F32),
        scratch_shapes=[pltpu.VMEM((tm, F), _BF16)],
        compiler_params=_params("parallel"),
        name="ffn_ln",
    )(x, wg, wu, wd, g, b)


def _proj_kernel(x_ref, w_ref, o_ref, *, tn, scaled, scale):
    xb = x_ref[...].astype(_BF16)
    for c in range(0, w_ref.shape[1], tn):
        y = _dot(xb, w_ref[:, c:c + tn])
        if c < scaled:
            y = y * scale
        o_ref[:, c:c + tn] = y.astype(o_ref.dtype)


def _proj(x, w, *, tm, tn, scaled=0, scale=1.0):
    M, K = x.shape
    N = w.shape[1]
    assert scaled % tn == 0
    return pl.pallas_call(
        functools.partial(_proj_kernel, tn=tn, scaled=scaled, scale=scale),
        grid=(M // tm,),
        in_specs=[pl.BlockSpec((tm, K), lambda i: (i, 0)),
                  pl.BlockSpec((K, N), lambda i: (0, 0), pipeline_mode=pl.Buffered(1))],
        out_specs=pl.BlockSpec((tm, N), lambda i: (i, 0)),
        out_shape=jax.ShapeDtypeStruct((M, N), _BF16),
        compiler_params=_params("parallel"),
        name="proj",
    )(x, w)


def _softmax2_pv(s, v):
    m = jnp.max(s, axis=-1, keepdims=True)
    p = jnp.exp2(s - m)
    l = jnp.sum(p, axis=-1, keepdims=True)
    return _dot(p.astype(_BF16), v) / l


_MIX_SLABS = 1


def _mix_mem_kernel(*refs, alpha, n_in, scale):
    a_refs = refs[:n_in]
    w_refs = refs[n_in:2 * n_in]
    x_ref, g1_ref, b1_ref, wq_ref, k_ref, v_ref, wo_ref, g2_ref, b2_ref, o_ref = refs[2 * n_in:]
    hd = x_ref.shape[1] // MEM_HEADS
    slab = x_ref.shape[0] // _MIX_SLABS
    for r in range(0, x_ref.shape[0], slab):
        rows = slice(r, r + slab)
        y = _dot(a_refs[0][rows, :], w_refs[0][...])
        for a_ref, w_ref in zip(a_refs[1:], w_refs[1:]):
            y = y + _dot(a_ref[rows, :], w_ref[...])
        x1 = _layer_norm(alpha * x_ref[rows, :] + y, g1_ref[...], b1_ref[...])
        q = (_dot(x1.astype(_BF16), wq_ref[...]) * scale).astype(_BF16)
        heads = []
        for h in range(MEM_HEADS):
            sl = slice(h * hd, (h + 1) * hd)
            heads.append(_softmax2_pv(_dot_nt(q[:, sl], k_ref[:, sl]), v_ref[:, sl]).astype(_BF16))
        y2 = _dot(jnp.concatenate(heads, axis=1), wo_ref[...])
        o_ref[rows, :] = _layer_norm(alpha * x1 + y2, g2_ref[...], b2_ref[...])


def _mix_mem(acts, ws, x, g1, b1, wq, kv, wo, g2, b2, *, alpha, batch, seq, tm):
    M, D = x.shape
    n_in = len(acts)
    mt = kv.shape[0] // batch
    nt = seq // tm
    row = lambda b, t: (b * nt + t, 0)
    const = lambda b, t: (0, 0)
    resident = pl.Buffered(1)
    in_specs = [pl.BlockSpec((tm, a.shape[1]), row) for a in acts]
    in_specs += [pl.BlockSpec(w.shape, const, pipeline_mode=resident) for w in ws]
    in_specs += [
        pl.BlockSpec((tm, D), row), pl.BlockSpec((1, D), const), pl.BlockSpec((1, D), const),
        pl.BlockSpec((D, D), const, pipeline_mode=resident),
        pl.BlockSpec((mt, D), lambda b, t: (b, 0)),
        pl.BlockSpec((mt, D), lambda b, t: (b, 1)),
        pl.BlockSpec((D, D), const, pipeline_mode=resident),
        pl.BlockSpec((1, D), const), pl.BlockSpec((1, D), const),
    ]
    return pl.pallas_call(
        functools.partial(_mix_mem_kernel, alpha=alpha, n_in=n_in, scale=(D // MEM_HEADS) ** -0.5 * _LOG2E),
        grid=(batch, nt),
        in_specs=in_specs,
        out_specs=pl.BlockSpec((tm, D), row),
        out_shape=jax.ShapeDtypeStruct((M, D), _F32),
        compiler_params=_params("parallel", "parallel"),
        name="mix_mem",
    )(*acts, *ws, x, g1, b1, wq, kv, kv, wo, g2, b2)


_AB_CHUNK = 256
_AB_LAYOUT = (("qa", 512), ("ka", 256), ("va", 256), ("qb", 512), ("kb", 256), ("vb", 256))


def _rope(y, cos, sin_lo, sin_hi):
    return y * cos + pltpu.roll(y, 96, axis=1) * sin_lo + pltpu.roll(y, 32, axis=1) * sin_hi


def _head_mean_square(y, bd):
    sq = y * y
    hi = sq.astype(_BF16)
    lo = (sq - hi.astype(_F32)).astype(_BF16)
    return _dot(hi, bd) + _dot(lo, bd)


def _ab_proj_kernel(x_ref, w_ref, bd_ref, qg_ref, kg_ref, c2_ref, sl2_ref, sh2_ref, c1_ref, sl1_ref, sh1_ref,
                    o_ref, *, scale):
    xb = x_ref[...].astype(_BF16)
    bd = bd_ref[...]
    col = 0
    for name, width in _AB_LAYOUT:
        for c in range(col, col + width, _AB_CHUNK):
            y2 = _dot(xb, w_ref[:, c:c + _AB_CHUNK])
            for h in range(0, _AB_CHUNK, LANES):
                y = y2[:, h:h + LANES]
                if name in ("qa", "ka"):
                    gain = qg_ref[...] if name == "qa" else kg_ref[...]
                    y = y * lax.rsqrt(_head_mean_square(y, bd) + RMS_EPS) * gain
                    y = _rope(y, c2_ref[...], sl2_ref[...], sh2_ref[...])
                elif name in ("qb", "kb"):
                    y = _rope(y, c1_ref[...], sl1_ref[...], sh1_ref[...])
                if name in ("qa", "qb"):
                    y = y * (scale * _LOG2E)
                o_ref[:, c + h:c + h + LANES] = y.astype(o_ref.dtype)
        col += width


def _ab_proj(x, w_ext, bd, qg, kg, tabs2, tabs1, *, seq, tm):
    M, D = x.shape
    N = w_ext.shape[1]
    nt = seq // tm
    tab_spec = pl.BlockSpec((tm, LANES), lambda i: (i % nt, 0))
    const = lambda shape: pl.BlockSpec(shape, lambda i: (0, 0))
    return pl.pallas_call(
        functools.partial(_ab_proj_kernel, scale=HEAD_DIM ** -0.5),
        grid=(M // tm,),
        in_specs=[pl.BlockSpec((tm, D), lambda i: (i, 0)), const((D, N)), const((LANES, LANES)),
                  const((1, LANES)), const((1, LANES))] + [tab_spec] * 6,
        out_specs=pl.BlockSpec((tm, N), lambda i: (i, 0)),
        out_shape=jax.ShapeDtypeStruct((M, N), _BF16),
        compiler_params=_params("parallel"),
        name="ab_proj",
    )(x, w_ext, bd, qg, kg, *tabs2, *tabs1)


def _stack_heads(q):
    t = q.shape[0]
    lo = lax.broadcasted_iota(jnp.int32, (t, LANES), 1) < HEAD_DIM
    zero = jnp.zeros((t, LANES), q.dtype)
    parts = []
    for c in range(0, q.shape[1], LANES):
        pair = q[:, c:c + LANES]
        parts += [jnp.where(lo, pair, zero), jnp.where(lo, zero, pair)]
    return jnp.concatenate(parts, axis=0)


def _unstack_heads(o, t):
    lo = lax.broadcasted_iota(jnp.int32, (t, LANES), 1) < HEAD_DIM
    n = o.shape[0] // t
    return jnp.concatenate(
        [jnp.where(lo, o[(2 * i) * t:(2 * i + 1) * t], o[(2 * i + 1) * t:(2 * i + 2) * t]) for i in range(n // 2)],
        axis=1)


def _attn_a_kernel(q_ref, k_ref, v_ref, o_ref, *, kc):
    tq = q_ref.shape[0]
    qs = _stack_heads(q_ref[...])
    m = l = acc = None
    for c in range(0, k_ref.shape[0], kc):
        s = _dot_nt(qs, k_ref[c:c + kc, :])
        mc = jnp.max(s, axis=-1, keepdims=True)
        m_new = mc if m is None else jnp.maximum(m, mc)
        p = jnp.exp2(s - m_new)
        lc = p[:, :LANES]
        for i in range(LANES, kc, LANES):
            lc = lc + p[:, i:i + LANES]
        pv = _dot(p.astype(_BF16), v_ref[c:c + kc, :])
        if m is None:
            l, acc = lc, pv
        else:
            corr = jnp.exp2(m - m_new)
            l, acc = l * corr + lc, acc * corr + pv
        m = m_new
    o_ref[...] = _unstack_heads(acc / jnp.sum(l, axis=-1, keepdims=True), tq).astype(o_ref.dtype)


def _attn_a_bounded_kernel(q_ref, k_ref, v_ref, o_ref, *, kc):
    tq = q_ref.shape[0]
    qs = _stack_heads(q_ref[...])
    l = acc = None
    for c in range(0, k_ref.shape[0], kc):
        p = jnp.exp2(_dot_nt(qs, k_ref[c:c + kc, :]))
        lc = p[:, :LANES]
        for i in range(LANES, kc, LANES):
            lc = lc + p[:, i:i + LANES]
        pv = _dot(p.astype(_BF16), v_ref[c:c + kc, :])
        l, acc = (lc, pv) if l is None else (l + lc, acc + pv)
    o_ref[...] = _unstack_heads(acc / jnp.sum(l, axis=-1, keepdims=True), tq).astype(o_ref.dtype)


A_SCORE_BOUND = 60.0


def _a_score_bound(q_gain, k_gain):
    return 1.02 * HEAD_DIM ** 0.5 * _LOG2E * jnp.max(jnp.abs(q_gain)) * jnp.max(jnp.abs(k_gain))


def _attn_a(qkv, *, batch, seq, tq, kc, bounded):
    M = qkv.shape[0]
    nt = seq // tq
    gw = (A_HEADS // A_KV_HEADS) * HEAD_DIM
    k_col = A_HEADS * HEAD_DIM // LANES
    v_col = k_col + A_KV_HEADS
    return pl.pallas_call(
        functools.partial(_attn_a_bounded_kernel if bounded else _attn_a_kernel, kc=kc),
        grid=(batch, A_KV_HEADS, nt),
        in_specs=[
            pl.BlockSpec((tq, gw), lambda b, j, t: (b * nt + t, j)),
            pl.BlockSpec((seq, LANES), lambda b, j, t: (b, k_col + j)),
            pl.BlockSpec((seq, LANES), lambda b, j, t: (b, v_col + j)),
        ],
        out_specs=pl.BlockSpec((tq, gw), lambda b, j, t: (b * nt + t, j)),
        out_shape=jax.ShapeDtypeStruct((M, A_HEADS * HEAD_DIM), _BF16),
        compiler_params=_params("parallel", "parallel", "parallel"),
        name="attn_a",
    )(qkv, qkv, qkv)


def _b_band_masks():
    g = B_HEADS // B_KV_HEADS
    qoff = (np.arange(g * Q_BLOCK) % Q_BLOCK)[:, None]
    koff = np.arange(Q_BLOCK + 2 * WINDOW)[None, :]
    return np.stack([np.where(np.abs(koff - d - qoff) <= WINDOW, 0.0, -np.inf) for d in (0, WINDOW, 2 * WINDOW)]
                    ).astype(np.float32)


def _attn_b_kernel(sink_ref, mask_ref, q_ref, k_ref, v_ref, o_ref, *, seq):
    j = pl.program_id(1)
    t = pl.program_id(2)
    qb = Q_BLOCK
    g = B_HEADS // B_KV_HEADS
    band = qb + 2 * WINDOW
    nsub = q_ref.shape[0] // qb
    hrow = lax.broadcasted_iota(jnp.int32, (g * qb, 1), 0) // qb
    sink = jnp.zeros((g * qb, 1), _F32)
    for h in range(g):
        sink = jnp.where(hrow == h, sink_ref[j * g + h] * _LOG2E, sink)
    for u in range(nsub):
        blk = t * nsub + u
        q0 = blk * qb
        start = pl.multiple_of(jnp.clip(q0 - WINDOW, 0, seq - band), LANES)
        kind = jnp.where(blk == 0, 0, jnp.where(blk == seq // qb - 1, 2, 1))
        kb = k_ref[pl.ds(start, band), :]
        vb = v_ref[pl.ds(start, band), :]
        qs = _stack_heads(q_ref[u * qb:(u + 1) * qb, :])
        s = _dot_nt(qs, kb) + mask_ref[kind]
        m = jnp.maximum(jnp.max(s, axis=-1, keepdims=True), sink)
        p = jnp.exp2(s - m)
        l = jnp.sum(p, axis=-1, keepdims=True) + jnp.exp2(sink - m)
        o = _dot(p.astype(_BF16), vb) / l
        o_ref[u * qb:(u + 1) * qb, :] = _unstack_heads(o, qb).astype(o_ref.dtype)


def _attn_b(qkv, sink, *, batch, seq, tq):
    M = qkv.shape[0]
    nt = seq // tq
    assert tq % Q_BLOCK == 0 and seq // Q_BLOCK >= 2
    masks = jnp.asarray(_b_band_masks())
    gw = (B_HEADS // B_KV_HEADS) * HEAD_DIM
    base = (A_HEADS + 4 * A_KV_HEADS) * HEAD_DIM
    q_col = base // gw
    k_col = (base + B_HEADS * HEAD_DIM) // LANES
    v_col = k_col + B_KV_HEADS
    return pl.pallas_call(
        functools.partial(_attn_b_kernel, seq=seq),
        grid=(batch, B_KV_HEADS, nt),
        in_specs=[
            pl.BlockSpec(memory_space=pltpu.SMEM),
            pl.BlockSpec(masks.shape, lambda b, j, t: (0, 0, 0), pipeline_mode=pl.Buffered(1)),
            pl.BlockSpec((tq, gw), lambda b, j, t: (b * nt + t, q_col + j)),
            pl.BlockSpec((seq, LANES), lambda b, j, t: (b, k_col + j)),
            pl.BlockSpec((seq, LANES), lambda b, j, t: (b, v_col + j)),
        ],
        out_specs=pl.BlockSpec((tq, gw), lambda b, j, t: (b * nt + t, j)),
        out_shape=jax.ShapeDtypeStruct((M, B_HEADS * HEAD_DIM), _BF16),
        compiler_params=_params("parallel", "parallel", "parallel"),
        name="attn_b",
    )(sink, masks, qkv, qkv, qkv)


C_GROUP = 4
C_BAND = C_GROUP + NA_KH
_C_GROUP_UNROLL = 16


def _c_band_start(g, rows):
    return jnp.clip(g * C_GROUP - NA_KH // 2, 0, rows - C_BAND)


def _attn_c_kernel(q_ref, k_ref, v_ref, bias_ref, o_ref, *, rows):
    nq = C_GROUP * GRID_W
    nk = C_BAND * GRID_W
    groups = rows // C_GROUP

    def body(g, carry):
        q0 = pl.multiple_of(g * nq, nq)
        k0 = pl.multiple_of(_c_band_start(g, rows) * GRID_W, GRID_W)
        kind = jnp.where(g == 0, 0, jnp.where(g == groups - 1, 2, 1))
        qs = _stack_heads(q_ref[pl.ds(q0, nq), :])
        s = _dot_nt(qs, k_ref[pl.ds(k0, nk), :])
        o = _softmax2_pv(s + bias_ref[kind], v_ref[pl.ds(k0, nk), :])
        o_ref[pl.ds(q0, nq), :] = _unstack_heads(o, nq).astype(o_ref.dtype)
        return carry

    lax.fori_loop(0, groups, body, 0, unroll=_C_GROUP_UNROLL)


def _attn_c(qkv, bias, *, batch, seq):
    M = qkv.shape[0]
    width = qkv.shape[1] // 3
    npair = width // LANES
    rows = seq // GRID_W
    return pl.pallas_call(
        functools.partial(_attn_c_kernel, rows=rows),
        grid=(batch, npair),
        in_specs=[
            pl.BlockSpec((seq, LANES), lambda b, p: (b, p)),
            pl.BlockSpec((seq, LANES), lambda b, p: (b, npair + p)),
            pl.BlockSpec((seq, LANES), lambda b, p: (b, 2 * npair + p)),
            pl.BlockSpec((None,) + bias.shape[1:], lambda b, p: (p, 0, 0, 0)),
        ],
        out_specs=pl.BlockSpec((seq, LANES), lambda b, p: (b, p)),
        out_shape=jax.ShapeDtypeStruct((M, width), _BF16),
        compiler_params=_params("parallel", "parallel"),
        name="attn_c",
    )(qkv, qkv, qkv, bias)


def _c_window(kind, u):
    if kind == 0:
        return 0, NA_KH, 0
    if kind == 1:
        return u, u + NA_KH, -(NA_KH // 2)
    return C_GROUP, C_GROUP + NA_KH, C_GROUP - C_BAND


def _c_bias_kernel(rpb_ref, o_ref):
    nq = C_GROUP * GRID_W
    w = lax.broadcasted_iota(jnp.int32, (GRID_W, LANES), 0)
    lane = lax.broadcasted_iota(jnp.int32, (GRID_W, LANES), 1)
    kc = jnp.bitwise_and(lane, GRID_W - 1)
    upper = lane >= GRID_W
    cs = jnp.clip(w - NA_KW // 2, 0, GRID_W - NA_KW)
    inside = (kc >= cs) & (kc < cs + NA_KW)
    for hh in range(2):
        lo_t, hi_t = [], []
        for i in range(2 * NA_KH - 1):
            r = jnp.broadcast_to(rpb_ref[hh, i:i + 1, :], (GRID_W, LANES))
            lo_t.append(pltpu.roll(r, LANES - (NA_KW - 1), 1, stride=1, stride_axis=0))
            hi_t.append(pltpu.roll(r, GRID_W - (NA_KW - 1), 1, stride=1, stride_axis=0))
        neg = jnp.full((GRID_W, LANES), -jnp.inf, _F32)
        for kind in range(3):
            for u in range(C_GROUP):
                lo, hi, shift = _c_window(kind, u)
                for b2 in range(C_BAND // 2):
                    halves = []
                    for b, table in ((2 * b2, lo_t), (2 * b2 + 1, hi_t)):
                        halves.append(table[shift + b - u + NA_KH - 1] if lo <= b < hi else neg)
                    tile = jnp.where(inside, jnp.where(upper, halves[1], halves[0]) * _LOG2E, neg)
                    r0 = hh * nq + u * GRID_W
                    o_ref[kind, r0:r0 + GRID_W, b2 * LANES:(b2 + 1) * LANES] = tile


def _c_bias_table(rpb):
    H, nr, nc = rpb.shape
    assert nr == 2 * NA_KH - 1 and nc == 2 * NA_KW - 1 and 2 * GRID_W == LANES and H % 2 == 0 and C_BAND % 2 == 0
    rpb_pad = jnp.pad(rpb, ((0, 0), (0, 0), (0, LANES - nc)))
    shape = (3, 2 * C_GROUP * GRID_W, C_BAND * GRID_W)
    return pl.pallas_call(
        _c_bias_kernel,
        grid=(H // 2,),
        in_specs=[pl.BlockSpec((2, nr, LANES), lambda p: (p, 0, 0))],
        out_specs=pl.BlockSpec((None,) + shape, lambda p: (p, 0, 0, 0)),
        out_shape=jax.ShapeDtypeStruct((H // 2,) + shape, _F32),
        compiler_params=_params("parallel"),
        name="c_bias",
    )(rpb_pad)


def _rope_tables(ang):
    half = HEAD_DIM // 2
    cos, sin = jnp.cos(ang), jnp.sin(ang)
    zero = jnp.zeros_like(sin)
    cos_h = jnp.concatenate([cos, cos], axis=-1)
    sin_lo = jnp.concatenate([-sin, zero], axis=-1)
    sin_hi = jnp.concatenate([zero, sin], axis=-1)
    assert cos_h.shape[-1] == 2 * half
    return tuple(jnp.concatenate([t, t], axis=-1).astype(_F32) for t in (cos_h, sin_lo, sin_hi))


def _rope_angles(pos, dim):
    inv = ROPE_THETA ** (-jnp.arange(0, dim, 2, dtype=_F32) / dim)
    return pos[:, None] * inv[None, :]


def _ab_columns():
    d = HEAD_DIM
    cols, c = [], 0
    for n_q, n_kv in ((A_HEADS, A_KV_HEADS), (B_HEADS, B_KV_HEADS)):
        cols += list(range(c, c + n_q * d))
        c += n_q * d
        for _ in range(2):
            for h in range(n_kv):
                cols += 2 * list(range(c + h * d, c + (h + 1) * d))
            c += n_kv * d
    return np.asarray(cols, dtype=np.int32)


def _pick(n, pref):
    t = min(n, pref)
    assert n % t == 0, (n, pref)
    return t


def kernel(x, mem, ln_g, ln_b, ffn_w_gate, ffn_w_up, ffn_w_down, ab_w_in, ab_w_out, ab_q_gain, ab_k_gain, ab_sink,
           c_w_in, c_w_out, c_rpb, mem_w_q, mem_w_kv, mem_w_o):
    B, S, D = x.shape
    depth = ln_g.shape[0]
    M = B * S
    MT = mem.shape[1]
    rows = S // GRID_W
    assert S % GRID_W == 0 and rows >= C_BAND and rows % C_GROUP == 0 and NA_KH // 2 == C_GROUP
    assert S % Q_BLOCK == 0 and S >= Q_BLOCK + 2 * WINDOW
    assert D % (MEM_HEADS * LANES) == 0 and D == 2 * A_HEADS * HEAD_DIM
    alpha = (2.0 * depth) ** 0.25

    t = jnp.arange(S)
    row = (t // GRID_W).astype(_F32)
    colp = (t % GRID_W).astype(_F32)
    ang_2d = jnp.concatenate([_rope_angles(row, HEAD_DIM // 2), _rope_angles(colp, HEAD_DIM // 2)], axis=-1)
    ang_1d = _rope_angles(t.astype(_F32), HEAD_DIM)
    tabs2 = _rope_tables(ang_2d)
    tabs1 = _rope_tables(ang_1d)
    lane_head = np.arange(LANES) // HEAD_DIM
    bd = jnp.asarray((lane_head[:, None] == lane_head[None, :]) / HEAD_DIM, dtype=_BF16)

    wg = ffn_w_gate.astype(_BF16)
    wu = ffn_w_up.astype(_BF16)
    wd = ffn_w_down.astype(_BF16)
    ab_cols = _ab_columns()

    tm_ffn = _pick(M, 1024)
    tf = _pick(wg.shape[-1], 256)
    tm = _pick(M, 1024)
    tm_ab = _pick(S, 1024)

    xf = x.reshape(M, D)
    memf = mem.reshape(B * MT, D)
    for i in range(depth):
        j = i // 2
        g = lambda k: ln_g[i, k].reshape(1, D)
        bb = lambda k: ln_b[i, k].reshape(1, D)
        xf = _ffn_ln(xf, wg, wu, wd, g(0), bb(0), i, 0, alpha=alpha, tm=tm_ffn, tf=tf)
        if i % 2 == 0:
            w_ext = ab_w_in[j][:, ab_cols].astype(_BF16)
            qg = jnp.tile(ab_q_gain[j], LANES // HEAD_DIM).reshape(1, LANES)
            kg = jnp.tile(ab_k_gain[j], LANES // HEAD_DIM).reshape(1, LANES)
            qkv = _ab_proj(xf, w_ext, bd, qg, kg, tabs2, tabs1, seq=S, tm=tm_ab)
            attn_a = functools.partial(_attn_a, batch=B, seq=S, tq=_pick(S, 512))
            out_a = lax.cond(_a_score_bound(ab_q_gain[j], ab_k_gain[j]) <= A_SCORE_BOUND,
                             functools.partial(attn_a, kc=_pick(S, 256), bounded=True),
                             functools.partial(attn_a, kc=_pick(S, 512), bounded=False), qkv)
            out_b = _attn_b(qkv, ab_sink[j], batch=B, seq=S, tq=_pick(S, 8 * Q_BLOCK))
            w_out = ab_w_out[j].astype(_BF16)
            na = A_HEADS * HEAD_DIM
            acts, ws = [out_a, out_b], [w_out[:na], w_out[na:]]
        else:
            width = c_w_in.shape[-1] // 3
            qkv = _proj(xf, c_w_in[j].astype(_BF16), tm=tm, tn=_pick(width, 512),
                        scaled=width, scale=HEAD_DIM ** -0.5 * _LOG2E)
            acts, ws = [_attn_c(qkv, _c_bias_table(c_rpb[j]), batch=B, seq=S)], [c_w_out[j].astype(_BF16)]
        kv = _proj(memf, mem_w_kv[i].astype(_BF16), tm=_pick(B * MT, 512), tn=_pick(2 * D, 512))
        xf = _mix_mem(acts, ws, xf, g(1), bb(1), mem_w_q[i].astype(_BF16), kv, mem_w_o[i].astype(_BF16), g(2), bb(2),
                      alpha=alpha, batch=B, seq=S, tm=tm_ab)
        xf = _ffn_ln(xf, wg, wu, wd, g(3), bb(3), i, 1, alpha=alpha, tm=tm_ffn, tf=tf)
    return xf.reshape(B, S, D)
```

```python
import functools

import numpy as np
import jax
import jax.numpy as jnp
from jax import lax
from jax.experimental import pallas as pl
from jax.experimental.pallas import tpu as pltpu

GRID_W = 64
HEAD_DIM = 64
A_HEADS = 8
A_KV_HEADS = 2
B_HEADS = 8
B_KV_HEADS = 2
Q_BLOCK = 128
WINDOW = 128
NA_KH = 8
NA_KW = 16
MEM_HEADS = 4
ROPE_THETA = 10000.0
LN_EPS = 1e-5
RMS_EPS = 1e-6

LANES = 128
VMEM_LIMIT_BYTES = 48 * 1024 * 1024

_BF16 = jnp.bfloat16
_F32 = jnp.float32
_NT = (((1,), (1,)), ((), ()))
_LOG2E = 1.4426950408889634


def _dot(a, b):
    return jnp.dot(a, b, preferred_element_type=_F32)


def _dot_nt(a, b):
    return lax.dot_general(a, b, _NT, preferred_element_type=_F32)


def _params(*sem):
    return pltpu.CompilerParams(dimension_semantics=sem, vmem_limit_bytes=VMEM_LIMIT_BYTES)


def _layer_norm(z, g, b):
    mu = jnp.mean(z, axis=-1, keepdims=True)
    zc = z - mu
    var = jnp.mean(zc * zc, axis=-1, keepdims=True)
    return zc * lax.rsqrt(var + LN_EPS) * g + b


_FFN_OUT_SLABS = 2


def _ffn_ln_kernel(x_ref, wg_ref, wu_ref, wd_ref, g_ref, b_ref, o_ref, act_ref, *, alpha, tf):
    xb = x_ref[...].astype(_BF16)
    for c in range(0, wg_ref.shape[1], tf):
        gate = _dot(xb, wg_ref[:, c:c + tf])
        up = _dot(xb, wu_ref[:, c:c + tf])
        act_ref[:, c:c + tf] = (gate * jax.nn.sigmoid(gate) * up).astype(_BF16)
    slab = x_ref.shape[0] // _FFN_OUT_SLABS
    for r in range(0, x_ref.shape[0], slab):
        y = _dot(act_ref[r:r + slab, :], wd_ref[...])
        o_ref[r:r + slab, :] = _layer_norm(alpha * x_ref[r:r + slab, :] + 0.5 * y, g_ref[...], b_ref[...])


def _ffn_ln(x, wg, wu, wd, g, b, layer, half, *, alpha, tm, tf):
    M, D = x.shape
    F = wg.shape[-1]
    resident = pl.Buffered(1)
    return pl.pallas_call(
        functools.partial(_ffn_ln_kernel, alpha=alpha, tf=tf),
        grid=(M // tm,),
        in_specs=[
            pl.BlockSpec((tm, D), lambda i: (i, 0)),
            pl.BlockSpec((None, None, D, F), lambda i: (layer, half, 0, 0), pipeline_mode=resident),
            pl.BlockSpec((None, None, D, F), lambda i: (layer, half, 0, 0), pipeline_mode=resident),
            pl.BlockSpec((None, None, F, D), lambda i: (layer, half, 0, 0), pipeline_mode=resident),
            pl.BlockSpec((1, D), lambda i: (0, 0)),
            pl.BlockSpec((1, D), lambda i: (0, 0)),
        ],
        out_specs=pl.BlockSpec((tm, D), lambda i: (i, 0)),
        out_shape=jax.ShapeDtypeStruct((M, D), _F32),
        scratch_shapes=[pltpu.VMEM((tm, F), _BF16)],
        compiler_params=_params("parallel"),
        name="ffn_ln",
    )(x, wg, wu, wd, g, b)


def _proj_kernel(x_ref, w_ref, o_ref, *, tn, scaled, scale):
    xb = x_ref[...].astype(_BF16)
    for c in range(0, w_ref.shape[1], tn):
        y = _dot(xb, w_ref[:, c:c + tn])
        if c < scaled:
            y = y * scale
        o_ref[:, c:c + tn] = y.astype(o_ref.dtype)


def _proj(x, w, *, tm, tn, scaled=0, scale=1.0):
    M, K = x.shape
    N = w.shape[1]
    assert scaled % tn == 0
    return pl.pallas_call(
        functools.partial(_proj_kernel, tn=tn, scaled=scaled, scale=scale),
        grid=(M // tm,),
        in_specs=[pl.BlockSpec((tm, K), lambda i: (i, 0)),
                  pl.BlockSpec((K, N), lambda i: (0, 0), pipeline_mode=pl.Buffered(1))],
        out_specs=pl.BlockSpec((tm, N), lambda i: (i, 0)),
        out_shape=jax.ShapeDtypeStruct((M, N), _BF16),
        compiler_params=_params("parallel"),
        name="proj",
    )(x, w)


def _softmax2_pv(s, v):
    m = jnp.max(s, axis=-1, keepdims=True)
    p = jnp.exp2(s - m)
    l = jnp.sum(p, axis=-1, keepdims=True)
    return _dot(p.astype(_BF16), v) / l


def _mix_mem_kernel(*refs, alpha, n_in, scale):
    a_refs = refs[:n_in]
    w_refs = refs[n_in:2 * n_in]
    x_ref, g1_ref, b1_ref, wq_ref, k_ref, v_ref, wo_ref, g2_ref, b2_ref, o_ref = refs[2 * n_in:]
    hd = x_ref.shape[1] // MEM_HEADS
    y = _dot(a_refs[0][...], w_refs[0][...])
    for a_ref, w_ref in zip(a_refs[1:], w_refs[1:]):
        y = y + _dot(a_ref[...], w_ref[...])
    x1 = _layer_norm(alpha * x_ref[...] + y, g1_ref[...], b1_ref[...])
    q = (_dot(x1.astype(_BF16), wq_ref[...]) * scale).astype(_BF16)
    heads = []
    for h in range(MEM_HEADS):
        sl = slice(h * hd, (h + 1) * hd)
        heads.append(_softmax2_pv(_dot_nt(q[:, sl], k_ref[:, sl]), v_ref[:, sl]).astype(_BF16))
    y2 = _dot(jnp.concatenate(heads, axis=1), wo_ref[...])
    o_ref[...] = _layer_norm(alpha * x1 + y2, g2_ref[...], b2_ref[...])


def _mix_mem(acts, ws, x, g1, b1, wq, kv, wo, g2, b2, *, alpha, batch, seq, tm):
    M, D = x.shape
    n_in = len(acts)
    mt = kv.shape[0] // batch
    nt = seq // tm
    row = lambda b, t: (b * nt + t, 0)
    const = lambda b, t: (0, 0)
    resident = pl.Buffered(1)
    in_specs = [pl.BlockSpec((tm, a.shape[1]), row) for a in acts]
    in_specs += [pl.BlockSpec(w.shape, const, pipeline_mode=resident) for w in ws]
    in_specs += [
        pl.BlockSpec((tm, D), row), pl.BlockSpec((1, D), const), pl.BlockSpec((1, D), const),
        pl.BlockSpec((D, D), const, pipeline_mode=resident),
        pl.BlockSpec((mt, D), lambda b, t: (b, 0)),
        pl.BlockSpec((mt, D), lambda b, t: (b, 1)),
        pl.BlockSpec((D, D), const, pipeline_mode=resident),
        pl.BlockSpec((1, D), const), pl.BlockSpec((1, D), const),
    ]
    return pl.pallas_call(
        functools.partial(_mix_mem_kernel, alpha=alpha, n_in=n_in, scale=(D // MEM_HEADS) ** -0.5 * _LOG2E),
        grid=(batch, nt),
        in_specs=in_specs,
        out_specs=pl.BlockSpec((tm, D), row),
        out_shape=jax.ShapeDtypeStruct((M, D), _F32),
        compiler_params=_params("parallel", "parallel"),
        name="mix_mem",
    )(*acts, *ws, x, g1, b1, wq, kv, kv, wo, g2, b2)


_AB_CHUNK = 256
_AB_LAYOUT = (("qa", 512), ("ka", 256), ("va", 256), ("qb", 512), ("kb", 256), ("vb", 256))


def _rope(y, cos, sin_lo, sin_hi):
    return y * cos + pltpu.roll(y, 96, axis=1) * sin_lo + pltpu.roll(y, 32, axis=1) * sin_hi


def _head_mean_square(y, bd):
    sq = y * y
    hi = sq.astype(_BF16)
    lo = (sq - hi.astype(_F32)).astype(_BF16)
    return _dot(hi, bd) + _dot(lo, bd)


def _ab_proj_kernel(x_ref, w_ref, bd_ref, qg_ref, kg_ref, c2_ref, sl2_ref, sh2_ref, c1_ref, sl1_ref, sh1_ref,
                    o_ref, *, scale):
    xb = x_ref[...].astype(_BF16)
    bd = bd_ref[...]
    col = 0
    for name, width in _AB_LAYOUT:
        for c in range(col, col + width, _AB_CHUNK):
            y2 = _dot(xb, w_ref[:, c:c + _AB_CHUNK])
            for h in range(0, _AB_CHUNK, LANES):
                y = y2[:, h:h + LANES]
                if name in ("qa", "ka"):
                    gain = qg_ref[...] if name == "qa" else kg_ref[...]
                    y = y * lax.rsqrt(_head_mean_square(y, bd) + RMS_EPS) * gain
                    y = _rope(y, c2_ref[...], sl2_ref[...], sh2_ref[...])
                elif name in ("qb", "kb"):
                    y = _rope(y, c1_ref[...], sl1_ref[...], sh1_ref[...])
                if name in ("qa", "qb"):
                    y = y * (scale * _LOG2E)
                o_ref[:, c + h:c + h + LANES] = y.astype(o_ref.dtype)
        col += width


def _ab_proj(x, w_ext, bd, qg, kg, tabs2, tabs1, *, seq, tm):
    M, D = x.shape
    N = w_ext.shape[1]
    nt = seq // tm
    tab_spec = pl.BlockSpec((tm, LANES), lambda i: (i % nt, 0))
    const = lambda shape: pl.BlockSpec(shape, lambda i: (0, 0))
    return pl.pallas_call(
        functools.partial(_ab_proj_kernel, scale=HEAD_DIM ** -0.5),
        grid=(M // tm,),
        in_specs=[pl.BlockSpec((tm, D), lambda i: (i, 0)), const((D, N)), const((LANES, LANES)),
                  const((1, LANES)), const((1, LANES))] + [tab_spec] * 6,
        out_specs=pl.BlockSpec((tm, N), lambda i: (i, 0)),
        out_shape=jax.ShapeDtypeStruct((M, N), _BF16),
        compiler_params=_params("parallel"),
        name="ab_proj",
    )(x, w_ext, bd, qg, kg, *tabs2, *tabs1)


def _stack_heads(q):
    t = q.shape[0]
    lo = lax.broadcasted_iota(jnp.int32, (t, LANES), 1) < HEAD_DIM
    zero = jnp.zeros((t, LANES), q.dtype)
    parts = []
    for c in range(0, q.shape[1], LANES):
        pair = q[:, c:c + LANES]
        parts += [jnp.where(lo, pair, zero), jnp.where(lo, zero, pair)]
    return jnp.concatenate(parts, axis=0)


def _unstack_heads(o, t):
    lo = lax.broadcasted_iota(jnp.int32, (t, LANES), 1) < HEAD_DIM
    n = o.shape[0] // t
    return jnp.concatenate(
        [jnp.where(lo, o[(2 * i) * t:(2 * i + 1) * t], o[(2 * i + 1) * t:(2 * i + 2) * t]) for i in range(n // 2)],
        axis=1)


def _attn_a_kernel(q_ref, k_ref, v_ref, o_ref, *, kc):
    tq = q_ref.shape[0]
    qs = _stack_heads(q_ref[...])
    m = l = acc = None
    for c in range(0, k_ref.shape[0], kc):
        s = _dot_nt(qs, k_ref[c:c + kc, :])
        mc = jnp.max(s, axis=-1, keepdims=True)
        m_new = mc if m is None else jnp.maximum(m, mc)
        p = jnp.exp2(s - m_new)
        lc = p[:, :LANES]
        for i in range(LANES, kc, LANES):
            lc = lc + p[:, i:i + LANES]
        pv = _dot(p.astype(_BF16), v_ref[c:c + kc, :])
        if m is None:
            l, acc = lc, pv
        else:
            corr = jnp.exp2(m - m_new)
            l, acc = l * corr + lc, acc * corr + pv
        m = m_new
    o_ref[...] = _unstack_heads(acc / jnp.sum(l, axis=-1, keepdims=True), tq).astype(o_ref.dtype)


def _attn_a_bounded_kernel(q_ref, k_ref, v_ref, o_ref, *, kc):
    tq = q_ref.shape[0]
    qs = _stack_heads(q_ref[...])
    l = acc = None
    for c in range(0, k_ref.shape[0], kc):
        p = jnp.exp2(_dot_nt(qs, k_ref[c:c + kc, :]))
        lc = p[:, :LANES]
        for i in range(LANES, kc, LANES):
            lc = lc + p[:, i:i + LANES]
        pv = _dot(p.astype(_BF16), v_ref[c:c + kc, :])
        l, acc = (lc, pv) if l is None else (l + lc, acc + pv)
    o_ref[...] = _unstack_heads(acc / jnp.sum(l, axis=-1, keepdims=True), tq).astype(o_ref.dtype)


A_SCORE_BOUND = 60.0


def _a_score_bound(q_gain, k_gain):
    return 1.02 * HEAD_DIM ** 0.5 * _LOG2E * jnp.max(jnp.abs(q_gain)) * jnp.max(jnp.abs(k_gain))


def _attn_a(qkv, *, batch, seq, tq, kc, bounded):
    M = qkv.shape[0]
    nt = seq // tq
    gw = (A_HEADS // A_KV_HEADS) * HEAD_DIM
    k_col = A_HEADS * HEAD_DIM // LANES
    v_col = k_col + A_KV_HEADS
    return pl.pallas_call(
        functools.partial(_attn_a_bounded_kernel if bounded else _attn_a_kernel, kc=kc),
        grid=(batch, A_KV_HEADS, nt),
        in_specs=[
            pl.BlockSpec((tq, gw), lambda b, j, t: (b * nt + t, j)),
            pl.BlockSpec((seq, LANES), lambda b, j, t: (b, k_col + j)),
            pl.BlockSpec((seq, LANES), lambda b, j, t: (b, v_col + j)),
        ],
        out_specs=pl.BlockSpec((tq, gw), lambda b, j, t: (b * nt + t, j)),
        out_shape=jax.ShapeDtypeStruct((M, A_HEADS * HEAD_DIM), _BF16),
        compiler_params=_params("parallel", "parallel", "parallel"),
        name="attn_a",
    )(qkv, qkv, qkv)


def _b_band_masks():
    g = B_HEADS // B_KV_HEADS
    qoff = (np.arange(g * Q_BLOCK) % Q_BLOCK)[:, None]
    koff = np.arange(Q_BLOCK + 2 * WINDOW)[None, :]
    return np.stack([np.where(np.abs(koff - d - qoff) <= WINDOW, 0.0, -np.inf) for d in (0, WINDOW, 2 * WINDOW)]
                    ).astype(np.float32)


def _attn_b_kernel(sink_ref, mask_ref, q_ref, k_ref, v_ref, o_ref, *, seq):
    j = pl.program_id(1)
    t = pl.program_id(2)
    qb = Q_BLOCK
    g = B_HEADS // B_KV_HEADS
    band = qb + 2 * WINDOW
    nsub = q_ref.shape[0] // qb
    hrow = lax.broadcasted_iota(jnp.int32, (g * qb, 1), 0) // qb
    sink = jnp.zeros((g * qb, 1), _F32)
    for h in range(g):
        sink = jnp.where(hrow == h, sink_ref[j * g + h] * _LOG2E, sink)
    for u in range(nsub):
        blk = t * nsub + u
        q0 = blk * qb
        start = pl.multiple_of(jnp.clip(q0 - WINDOW, 0, seq - band), LANES)
        kind = jnp.where(blk == 0, 0, jnp.where(blk == seq // qb - 1, 2, 1))
        kb = k_ref[pl.ds(start, band), :]
        vb = v_ref[pl.ds(start, band), :]
        qs = _stack_heads(q_ref[u * qb:(u + 1) * qb, :])
        s = _dot_nt(qs, kb) + mask_ref[kind]
        m = jnp.maximum(jnp.max(s, axis=-1, keepdims=True), sink)
        p = jnp.exp2(s - m)
        l = jnp.sum(p, axis=-1, keepdims=True) + jnp.exp2(sink - m)
        o = _dot(p.astype(_BF16), vb) / l
        o_ref[u * qb:(u + 1) * qb, :] = _unstack_heads(o, qb).astype(o_ref.dtype)


def _attn_b(qkv, sink, *, batch, seq, tq):
    M = qkv.shape[0]
    nt = seq // tq
    assert tq % Q_BLOCK == 0 and seq // Q_BLOCK >= 2
    masks = jnp.asarray(_b_band_masks())
    gw = (B_HEADS // B_KV_HEADS) * HEAD_DIM
    base = (A_HEADS + 4 * A_KV_HEADS) * HEAD_DIM
    q_col = base // gw
    k_col = (base + B_HEADS * HEAD_DIM) // LANES
    v_col = k_col + B_KV_HEADS
    return pl.pallas_call(
        functools.partial(_attn_b_kernel, seq=seq),
        grid=(batch, B_KV_HEADS, nt),
        in_specs=[
            pl.BlockSpec(memory_space=pltpu.SMEM),
            pl.BlockSpec(masks.shape, lambda b, j, t: (0, 0, 0), pipeline_mode=pl.Buffered(1)),
            pl.BlockSpec((tq, gw), lambda b, j, t: (b * nt + t, q_col + j)),
            pl.BlockSpec((seq, LANES), lambda b, j, t: (b, k_col + j)),
            pl.BlockSpec((seq, LANES), lambda b, j, t: (b, v_col + j)),
        ],
        out_specs=pl.BlockSpec((tq, gw), lambda b, j, t: (b * nt + t, j)),
        out_shape=jax.ShapeDtypeStruct((M, B_HEADS * HEAD_DIM), _BF16),
        compiler_params=_params("parallel", "parallel", "parallel"),
        name="attn_b",
    )(sink, masks, qkv, qkv, qkv)


C_GROUP = 4
C_BAND = C_GROUP + NA_KH
_C_GROUP_UNROLL = 16


def _c_band_start(g, rows):
    return jnp.clip(g * C_GROUP - NA_KH // 2, 0, rows - C_BAND)


def _attn_c_kernel(q_ref, k_ref, v_ref, bias_ref, o_ref, *, rows):
    nq = C_GROUP * GRID_W
    nk = C_BAND * GRID_W
    groups = rows // C_GROUP

    def body(g, carry):
        q0 = pl.multiple_of(g * nq, nq)
        k0 = pl.multiple_of(_c_band_start(g, rows) * GRID_W, GRID_W)
        kind = jnp.where(g == 0, 0, jnp.where(g == groups - 1, 2, 1))
        qs = _stack_heads(q_ref[pl.ds(q0, nq), :])
        s = _dot_nt(qs, k_ref[pl.ds(k0, nk), :])
        o = _softmax2_pv(s + bias_ref[kind], v_ref[pl.ds(k0, nk), :])
        o_ref[pl.ds(q0, nq), :] = _unstack_heads(o, nq).astype(o_ref.dtype)
        return carry

    lax.fori_loop(0, groups, body, 0, unroll=_C_GROUP_UNROLL)


def _attn_c(qkv, bias, *, batch, seq):
    M = qkv.shape[0]
    width = qkv.shape[1] // 3
    npair = width // LANES
    rows = seq // GRID_W
    return pl.pallas_call(
        functools.partial(_attn_c_kernel, rows=rows),
        grid=(batch, npair),
        in_specs=[
            pl.BlockSpec((seq, LANES), lambda b, p: (b, p)),
            pl.BlockSpec((seq, LANES), lambda b, p: (b, npair + p)),
            pl.BlockSpec((seq, LANES), lambda b, p: (b, 2 * npair + p)),
            pl.BlockSpec((None,) + bias.shape[1:], lambda b, p: (p, 0, 0, 0)),
        ],
        out_specs=pl.BlockSpec((seq, LANES), lambda b, p: (b, p)),
        out_shape=jax.ShapeDtypeStruct((M, width), _BF16),
        compiler_params=_params("parallel", "parallel"),
        name="attn_c",
    )(qkv, qkv, qkv, bias)


def _c_window(kind, u):
    if kind == 0:
        return 0, NA_KH, 0
    if kind == 1:
        return u, u + NA_KH, -(NA_KH // 2)
    return C_GROUP, C_GROUP + NA_KH, C_GROUP - C_BAND


def _c_bias_kernel(rpb_ref, o_ref):
    nq = C_GROUP * GRID_W
    w = lax.broadcasted_iota(jnp.int32, (GRID_W, LANES), 0)
    lane = lax.broadcasted_iota(jnp.int32, (GRID_W, LANES), 1)
    kc = jnp.bitwise_and(lane, GRID_W - 1)
    upper = lane >= GRID_W
    cs = jnp.clip(w - NA_KW // 2, 0, GRID_W - NA_KW)
    inside = (kc >= cs) & (kc < cs + NA_KW)
    for hh in range(2):
        lo_t, hi_t = [], []
        for i in range(2 * NA_KH - 1):
            r = jnp.broadcast_to(rpb_ref[hh, i:i + 1, :], (GRID_W, LANES))
            lo_t.append(pltpu.roll(r, LANES - (NA_KW - 1), 1, stride=1, stride_axis=0))
            hi_t.append(pltpu.roll(r, GRID_W - (NA_KW - 1), 1, stride=1, stride_axis=0))
        neg = jnp.full((GRID_W, LANES), -jnp.inf, _F32)
        for kind in range(3):
            for u in range(C_GROUP):
                lo, hi, shift = _c_window(kind, u)
                for b2 in range(C_BAND // 2):
                    halves = []
                    for b, table in ((2 * b2, lo_t), (2 * b2 + 1, hi_t)):
                        halves.append(table[shift + b - u + NA_KH - 1] if lo <= b < hi else neg)
                    tile = jnp.where(inside, jnp.where(upper, halves[1], halves[0]) * _LOG2E, neg)
                    r0 = hh * nq + u * GRID_W
                    o_ref[kind, r0:r0 + GRID_W, b2 * LANES:(b2 + 1) * LANES] = tile


def _c_bias_table(rpb):
    H, nr, nc = rpb.shape
    assert nr == 2 * NA_KH - 1 and nc == 2 * NA_KW - 1 and 2 * GRID_W == LANES and H % 2 == 0 and C_BAND % 2 == 0
    rpb_pad = jnp.pad(rpb, ((0, 0), (0, 0), (0, LANES - nc)))
    shape = (3, 2 * C_GROUP * GRID_W, C_BAND * GRID_W)
    return pl.pallas_call(
        _c_bias_kernel,
        grid=(H // 2,),
        in_specs=[pl.BlockSpec((2, nr, LANES), lambda p: (p, 0, 0))],
        out_specs=pl.BlockSpec((None,) + shape, lambda p: (p, 0, 0, 0)),
        out_shape=jax.ShapeDtypeStruct((H // 2,) + shape, _F32),
        compiler_params=_params("parallel"),
        name="c_bias",
    )(rpb_pad)


def _rope_tables(ang):
    half = HEAD_DIM // 2
    cos, sin = jnp.cos(ang), jnp.sin(ang)
    zero = jnp.zeros_like(sin)
    cos_h = jnp.concatenate([cos, cos], axis=-1)
    sin_lo = jnp.concatenate([-sin, zero], axis=-1)
    sin_hi = jnp.concatenate([zero, sin], axis=-1)
    assert cos_h.shape[-1] == 2 * half
    return tuple(jnp.concatenate([t, t], axis=-1).astype(_F32) for t in (cos_h, sin_lo, sin_hi))


def _rope_angles(pos, dim):
    inv = ROPE_THETA ** (-jnp.arange(0, dim, 2, dtype=_F32) / dim)
    return pos[:, None] * inv[None, :]


def _ab_columns():
    d = HEAD_DIM
    cols, c = [], 0
    for n_q, n_kv in ((A_HEADS, A_KV_HEADS), (B_HEADS, B_KV_HEADS)):
        cols += list(range(c, c + n_q * d))
        c += n_q * d
        for _ in range(2):
            for h in range(n_kv):
                cols += 2 * list(range(c + h * d, c + (h + 1) * d))
            c += n_kv * d
    return np.asarray(cols, dtype=np.int32)


def _pick(n, pref):
    t = min(n, pref)
    assert n % t == 0, (n, pref)
    return t


def _tiles(batch, seq, d_ff, mem_tokens):
    m = batch * seq
    return dict(
        ffn_rows=_pick(m, 512),
        ffn_cols=_pick(d_ff, 2 * LANES),
        proj_rows=_pick(m, 1024),
        proj_cols=2 * 2 * LANES,
        seq_rows=_pick(seq, 1024),
        mem_rows=_pick(batch * mem_tokens, 512),
        a_queries=_pick(seq, 512), a_keys_bounded=_pick(seq, 256), a_keys_running_max=_pick(seq, 512),
        b_queries=_pick(seq, 8 * Q_BLOCK),
    )


def kernel(x, mem, ln_g, ln_b, ffn_w_gate, ffn_w_up, ffn_w_down, ab_w_in, ab_w_out, ab_q_gain, ab_k_gain, ab_sink,
           c_w_in, c_w_out, c_rpb, mem_w_q, mem_w_kv, mem_w_o):
    B, S, D = x.shape
    depth = ln_g.shape[0]
    M = B * S
    MT = mem.shape[1]
    rows = S // GRID_W
    assert S % GRID_W == 0 and rows >= C_BAND and rows % C_GROUP == 0 and NA_KH // 2 == C_GROUP
    assert S % Q_BLOCK == 0 and S >= Q_BLOCK + 2 * WINDOW
    assert D % (MEM_HEADS * LANES) == 0 and D == 2 * A_HEADS * HEAD_DIM
    alpha = (2.0 * depth) ** 0.25

    t = jnp.arange(S)
    row = (t // GRID_W).astype(_F32)
    colp = (t % GRID_W).astype(_F32)
    ang_2d = jnp.concatenate([_rope_angles(row, HEAD_DIM // 2), _rope_angles(colp, HEAD_DIM // 2)], axis=-1)
    ang_1d = _rope_angles(t.astype(_F32), HEAD_DIM)
    tabs2 = _rope_tables(ang_2d)
    tabs1 = _rope_tables(ang_1d)
    lane_head = np.arange(LANES) // HEAD_DIM
    bd = jnp.asarray((lane_head[:, None] == lane_head[None, :]) / HEAD_DIM, dtype=_BF16)

    wg = ffn_w_gate.astype(_BF16)
    wu = ffn_w_up.astype(_BF16)
    wd = ffn_w_down.astype(_BF16)
    ab_cols = _ab_columns()

    tiles = _tiles(B, S, wg.shape[-1], MT)
    ffn = functools.partial(_ffn_ln, alpha=alpha, tm=tiles["ffn_rows"], tf=tiles["ffn_cols"])
    proj = functools.partial(_proj, tn=tiles["proj_cols"])

    xf = x.reshape(M, D)
    memf = mem.reshape(B * MT, D)
    for i in range(depth):
        j = i // 2
        g = lambda k: ln_g[i, k].reshape(1, D)
        bb = lambda k: ln_b[i, k].reshape(1, D)
        xf = ffn(xf, wg, wu, wd, g(0), bb(0), i, 0)
        if i % 2 == 0:
            w_ext = ab_w_in[j][:, ab_cols].astype(_BF16)
            qg = jnp.tile(ab_q_gain[j], LANES // HEAD_DIM).reshape(1, LANES)
            kg = jnp.tile(ab_k_gain[j], LANES // HEAD_DIM).reshape(1, LANES)
            qkv = _ab_proj(xf, w_ext, bd, qg, kg, tabs2, tabs1, seq=S, tm=tiles["seq_rows"])
            attn_a = functools.partial(_attn_a, batch=B, seq=S, tq=tiles["a_queries"])
            out_a = lax.cond(_a_score_bound(ab_q_gain[j], ab_k_gain[j]) <= A_SCORE_BOUND,
                             functools.partial(attn_a, kc=tiles["a_keys_bounded"], bounded=True),
                             functools.partial(attn_a, kc=tiles["a_keys_running_max"], bounded=False), qkv)
            out_b = _attn_b(qkv, ab_sink[j], batch=B, seq=S, tq=tiles["b_queries"])
            w_out = ab_w_out[j].astype(_BF16)
            na = A_HEADS * HEAD_DIM
            acts, ws = [out_a, out_b], [w_out[:na], w_out[na:]]
        else:
            width = c_w_in.shape[-1] // 3
            qkv = proj(xf, c_w_in[j].astype(_BF16), tm=tiles["proj_rows"], scaled=width,
                       scale=HEAD_DIM ** -0.5 * _LOG2E)
            acts, ws = [_attn_c(qkv, _c_bias_table(c_rpb[j]), batch=B, seq=S)], [c_w_out[j].astype(_BF16)]
        kv = proj(memf, mem_w_kv[i].astype(_BF16), tm=tiles["mem_rows"])
        xf = _mix_mem(acts, ws, xf, g(1), bb(1), mem_w_q[i].astype(_BF16), kv, mem_w_o[i].astype(_BF16), g(2), bb(2),
                      alpha=alpha, batch=B, seq=S, tm=tiles["seq_rows"])
        xf = ffn(xf, wg, wu, wd, g(3), bb(3), i, 1)
    return xf.reshape(B, S, D)
```

```python
import functools

import numpy as np
import jax
import jax.numpy as jnp
from jax import lax
from jax.experimental import pallas as pl
from jax.experimental.pallas import tpu as pltpu

GRID_W = 64
HEAD_DIM = 64
A_HEADS = 8
A_KV_HEADS = 2
B_HEADS = 8
B_KV_HEADS = 2
Q_BLOCK = 128
WINDOW = 128
NA_KH = 8
NA_KW = 16
MEM_HEADS = 4
ROPE_THETA = 10000.0
LN_EPS = 1e-5
RMS_EPS = 1e-6

LANES = 128
VMEM_LIMIT_BYTES = 48 * 1024 * 1024

_BF16 = jnp.bfloat16
_F32 = jnp.float32
_NT = (((1,), (1,)), ((), ()))
_LOG2E = 1.4426950408889634


def _dot(a, b):
    return jnp.dot(a, b, preferred_element_type=_F32)


def _dot_nt(a, b):
    return lax.dot_general(a, b, _NT, preferred_element_type=_F32)


def _params(*sem):
    return pltpu.CompilerParams(dimension_semantics=sem, vmem_limit_bytes=VMEM_LIMIT_BYTES)


def _layer_norm(z, g, b):
    mu = jnp.mean(z, axis=-1, keepdims=True)
    zc = z - mu
    var = jnp.mean(zc * zc, axis=-1, keepdims=True)
    return zc * lax.rsqrt(var + LN_EPS) * g + b


_FFN_OUT_SLABS = 2


def _ffn_ln_kernel(x_ref, wg_ref, wu_ref, wd_ref, g_ref, b_ref, o_ref, act_ref, *, alpha, tf):
    xb = x_ref[...].astype(_BF16)
    for c in range(0, wg_ref.shape[1], tf):
        gate = _dot(xb, wg_ref[:, c:c + tf])
        up = _dot(xb, wu_ref[:, c:c + tf])
        act_ref[:, c:c + tf] = (gate * jax.nn.sigmoid(gate) * up).astype(_BF16)
    slab = x_ref.shape[0] // _FFN_OUT_SLABS
    for r in range(0, x_ref.shape[0], slab):
        y = _dot(act_ref[r:r + slab, :], wd_ref[...])
        o_ref[r:r + slab, :] = _layer_norm(alpha * x_ref[r:r + slab, :] + 0.5 * y, g_ref[...], b_ref[...])


def _ffn_ln(x, wg, wu, wd, g, b, layer, half, *, alpha, tm, tf):
    M, D = x.shape
    F = wg.shape[-1]
    resident = pl.Buffered(1)
    return pl.pallas_call(
        functools.partial(_ffn_ln_kernel, alpha=alpha, tf=tf),
        grid=(M // tm,),
        in_specs=[
            pl.BlockSpec((tm, D), lambda i: (i, 0)),
            pl.BlockSpec((None, None, D, F), lambda i: (layer, half, 0, 0), pipeline_mode=resident),
            pl.BlockSpec((None, None, D, F), lambda i: (layer, half, 0, 0), pipeline_mode=resident),
            pl.BlockSpec((None, None, F, D), lambda i: (layer, half, 0, 0), pipeline_mode=resident),
            pl.BlockSpec((1, D), lambda i: (0, 0)),
            pl.BlockSpec((1, D), lambda i: (0, 0)),
        ],
        out_specs=pl.BlockSpec((tm, D), lambda i: (i, 0)),
        out_shape=jax.ShapeDtypeStruct((M, D), _F32),
        scratch_shapes=[pltpu.VMEM((tm, F), _BF16)],
        compiler_params=_params("parallel"),
        name="ffn_ln",
    )(x, wg, wu, wd, g, b)


def _proj_kernel(x_ref, w_ref, o_ref, *, tn, scaled, scale):
    xb = x_ref[...].astype(_BF16)
    for c in range(0, w_ref.shape[1], tn):
        y = _dot(xb, w_ref[:, c:c + tn])
        if c < scaled:
            y = y * scale
        o_ref[:, c:c + tn] = y.astype(o_ref.dtype)


def _proj(x, w, *, tm, tn, scaled=0, scale=1.0):
    M, K = x.shape
    N = w.shape[1]
    assert scaled % tn == 0
    return pl.pallas_call(
        functools.partial(_proj_kernel, tn=tn, scaled=scaled, scale=scale),
        grid=(M // tm,),
        in_specs=[pl.BlockSpec((tm, K), lambda i: (i, 0)),
                  pl.BlockSpec((K, N), lambda i: (0, 0), pipeline_mode=pl.Buffered(1))],
        out_specs=pl.BlockSpec((tm, N), lambda i: (i, 0)),
        out_shape=jax.ShapeDtypeStruct((M, N), _BF16),
        compiler_params=_params("parallel"),
        name="proj",
    )(x, w)


def _softmax2_pv(s, v):
    m = jnp.max(s, axis=-1, keepdims=True)
    p = jnp.exp2(s - m)
    l = jnp.sum(p, axis=-1, keepdims=True)
    return _dot(p.astype(_BF16), v) / l


def _mix_mem_kernel(*refs, alpha, n_in, scale):
    a_refs = refs[:n_in]
    w_refs = refs[n_in:2 * n_in]
    x_ref, g1_ref, b1_ref, wq_ref, k_ref, v_ref, wo_ref, g2_ref, b2_ref, o_ref = refs[2 * n_in:]
    hd = x_ref.shape[1] // MEM_HEADS
    y = _dot(a_refs[0][...], w_refs[0][...])
    for a_ref, w_ref in zip(a_refs[1:], w_refs[1:]):
        y = y + _dot(a_ref[...], w_ref[...])
    x1 = _layer_norm(alpha * x_ref[...] + y, g1_ref[...], b1_ref[...])
    q = (_dot(x1.astype(_BF16), wq_ref[...]) * scale).astype(_BF16)
    heads = []
    for h in range(MEM_HEADS):
        sl = slice(h * hd, (h + 1) * hd)
        heads.append(_softmax2_pv(_dot_nt(q[:, sl], k_ref[:, sl]), v_ref[:, sl]).astype(_BF16))
    y2 = _dot(jnp.concatenate(heads, axis=1), wo_ref[...])
    o_ref[...] = _layer_norm(alpha * x1 + y2, g2_ref[...], b2_ref[...])


def _mix_mem(acts, ws, x, g1, b1, wq, kv, wo, g2, b2, *, alpha, batch, seq, tm):
    M, D = x.shape
    n_in = len(acts)
    mt = kv.shape[0] // batch
    nt = seq // tm
    row = lambda b, t: (b * nt + t, 0)
    const = lambda b, t: (0, 0)
    resident = pl.Buffered(1)
    in_specs = [pl.BlockSpec((tm, a.shape[1]), row) for a in acts]
    in_specs += [pl.BlockSpec(w.shape, const, pipeline_mode=resident) for w in ws]
    in_specs += [
        pl.BlockSpec((tm, D), row), pl.BlockSpec((1, D), const), pl.BlockSpec((1, D), const),
        pl.BlockSpec((D, D), const, pipeline_mode=resident),
        pl.BlockSpec((mt, D), lambda b, t: (b, 0)),
        pl.BlockSpec((mt, D), lambda b, t: (b, 1)),
        pl.BlockSpec((D, D), const, pipeline_mode=resident),
        pl.BlockSpec((1, D), const), pl.BlockSpec((1, D), const),
    ]
    return pl.pallas_call(
        functools.partial(_mix_mem_kernel, alpha=alpha, n_in=n_in, scale=(D // MEM_HEADS) ** -0.5 * _LOG2E),
        grid=(batch, nt),
        in_specs=in_specs,
        out_specs=pl.BlockSpec((tm, D), row),
        out_shape=jax.ShapeDtypeStruct((M, D), _F32),
        compiler_params=_params("parallel", "parallel"),
        name="mix_mem",
    )(*acts, *ws, x, g1, b1, wq, kv, kv, wo, g2, b2)


_AB_CHUNK = 256
_AB_LAYOUT = (("qa", 512), ("ka", 256), ("va", 256), ("qb", 512), ("kb", 256), ("vb", 256))


def _rope(y, cos, sin_lo, sin_hi):
    return y * cos + pltpu.roll(y, 96, axis=1) * sin_lo + pltpu.roll(y, 32, axis=1) * sin_hi


def _head_mean_square(y, bd):
    sq = y * y
    hi = sq.astype(_BF16)
    lo = (sq - hi.astype(_F32)).astype(_BF16)
    return _dot(hi, bd) + _dot(lo, bd)


def _ab_proj_kernel(x_ref, w_ref, bd_ref, qg_ref, kg_ref, c2_ref, sl2_ref, sh2_ref, c1_ref, sl1_ref, sh1_ref,
                    o_ref, *, scale):
    xb = x_ref[...].astype(_BF16)
    bd = bd_ref[...]
    chunks, col = [], 0
    for name, width in _AB_LAYOUT:
        chunks += [(name, c) for c in range(col, col + width, _AB_CHUNK)]
        col += width
    normed = [nc for nc in chunks if nc[0] in ("qa", "ka")]
    plain = [nc for nc in chunks if nc[0] not in ("qa", "ka")]
    groups = range(0, _AB_CHUNK, LANES)

    def finish(name, c, h, y):
        if name in ("qa", "ka"):
            y = _rope(y, c2_ref[...], sl2_ref[...], sh2_ref[...])
        elif name in ("qb", "kb"):
            y = _rope(y, c1_ref[...], sl1_ref[...], sh1_ref[...])
        if name in ("qa", "qb"):
            y = y * (scale * _LOG2E)
        o_ref[:, c + h:c + h + LANES] = y.astype(o_ref.dtype)

    raw = {c: _dot(xb, w_ref[:, c:c + _AB_CHUNK]) for _, c in normed}
    ms = {(c, h): _head_mean_square(raw[c][:, h:h + LANES], bd) for _, c in normed for h in groups}
    for name, c in plain:
        y2 = _dot(xb, w_ref[:, c:c + _AB_CHUNK])
        for h in groups:
            finish(name, c, h, y2[:, h:h + LANES])
    for name, c in normed:
        gain = qg_ref[...] if name == "qa" else kg_ref[...]
        for h in groups:
            finish(name, c, h, raw[c][:, h:h + LANES] * lax.rsqrt(ms[(c, h)] + RMS_EPS) * gain)


def _ab_proj(x, w_ext, bd, qg, kg, tabs2, tabs1, *, seq, tm):
    M, D = x.shape
    N = w_ext.shape[1]
    nt = seq // tm
    tab_spec = pl.BlockSpec((tm, LANES), lambda i: (i % nt, 0))
    const = lambda shape: pl.BlockSpec(shape, lambda i: (0, 0))
    return pl.pallas_call(
        functools.partial(_ab_proj_kernel, scale=HEAD_DIM ** -0.5),
        grid=(M // tm,),
        in_specs=[pl.BlockSpec((tm, D), lambda i: (i, 0)), const((D, N)), const((LANES, LANES)),
                  const((1, LANES)), const((1, LANES))] + [tab_spec] * 6,
        out_specs=pl.BlockSpec((tm, N), lambda i: (i, 0)),
        out_shape=jax.ShapeDtypeStruct((M, N), _BF16),
        compiler_params=_params("parallel"),
        name="ab_proj",
    )(x, w_ext, bd, qg, kg, *tabs2, *tabs1)


def _stack_heads(q):
    t = q.shape[0]
    lo = lax.broadcasted_iota(jnp.int32, (t, LANES), 1) < HEAD_DIM
    zero = jnp.zeros((t, LANES), q.dtype)
    parts = []
    for c in range(0, q.shape[1], LANES):
        pair = q[:, c:c + LANES]
        parts += [jnp.where(lo, pair, zero), jnp.where(lo, zero, pair)]
    return jnp.concatenate(parts, axis=0)


def _unstack_heads(o, t):
    lo = lax.broadcasted_iota(jnp.int32, (t, LANES), 1) < HEAD_DIM
    n = o.shape[0] // t
    return jnp.concatenate(
        [jnp.where(lo, o[(2 * i) * t:(2 * i + 1) * t], o[(2 * i + 1) * t:(2 * i + 2) * t]) for i in range(n // 2)],
        axis=1)


def _attn_a_kernel(q_ref, k_ref, v_ref, o_ref, *, kc):
    tq = q_ref.shape[0]
    qs = _stack_heads(q_ref[...])
    m = l = acc = None
    for c in range(0, k_ref.shape[0], kc):
        s = _dot_nt(qs, k_ref[c:c + kc, :])
        mc = jnp.max(s, axis=-1, keepdims=True)
        m_new = mc if m is None else jnp.maximum(m, mc)
        p = jnp.exp2(s - m_new)
        lc = p[:, :LANES]
        for i in range(LANES, kc, LANES):
            lc = lc + p[:, i:i + LANES]
        pv = _dot(p.astype(_BF16), v_ref[c:c + kc, :])
        if m is None:
            l, acc = lc, pv
        else:
            corr = jnp.exp2(m - m_new)
            l, acc = l * corr + lc, acc * corr + pv
        m = m_new
    o_ref[...] = _unstack_heads(acc / jnp.sum(l, axis=-1, keepdims=True), tq).astype(o_ref.dtype)


def _attn_a_bounded_kernel(q_ref, k_ref, v_ref, o_ref, *, kc):
    tq = q_ref.shape[0]
    qs = _stack_heads(q_ref[...])
    l = acc = None
    for c in range(0, k_ref.shape[0], kc):
        p = jnp.exp2(_dot_nt(qs, k_ref[c:c + kc, :]))
        lc = p[:, :LANES]
        for i in range(LANES, kc, LANES):
            lc = lc + p[:, i:i + LANES]
        pv = _dot(p.astype(_BF16), v_ref[c:c + kc, :])
        l, acc = (lc, pv) if l is None else (l + lc, acc + pv)
    o_ref[...] = _unstack_heads(acc / jnp.sum(l, axis=-1, keepdims=True), tq).astype(o_ref.dtype)


A_SCORE_BOUND = 60.0


def _a_score_bound(q_gain, k_gain):
    return 1.02 * HEAD_DIM ** 0.5 * _LOG2E * jnp.max(jnp.abs(q_gain)) * jnp.max(jnp.abs(k_gain))


def _attn_a(qkv, *, batch, seq, tq, kc, bounded):
    M = qkv.shape[0]
    nt = seq // tq
    gw = (A_HEADS // A_KV_HEADS) * HEAD_DIM
    k_col = A_HEADS * HEAD_DIM // LANES
    v_col = k_col + A_KV_HEADS
    return pl.pallas_call(
        functools.partial(_attn_a_bounded_kernel if bounded else _attn_a_kernel, kc=kc),
        grid=(batch, A_KV_HEADS, nt),
        in_specs=[
            pl.BlockSpec((tq, gw), lambda b, j, t: (b * nt + t, j)),
            pl.BlockSpec((seq, LANES), lambda b, j, t: (b, k_col + j)),
            pl.BlockSpec((seq, LANES), lambda b, j, t: (b, v_col + j)),
        ],
        out_specs=pl.BlockSpec((tq, gw), lambda b, j, t: (b * nt + t, j)),
        out_shape=jax.ShapeDtypeStruct((M, A_HEADS * HEAD_DIM), _BF16),
        compiler_params=_params("parallel", "parallel", "parallel"),
        name="attn_a",
    )(qkv, qkv, qkv)


def _b_band_masks():
    g = B_HEADS // B_KV_HEADS
    qoff = (np.arange(g * Q_BLOCK) % Q_BLOCK)[:, None]
    koff = np.arange(Q_BLOCK + 2 * WINDOW)[None, :]
    return np.stack([np.where(np.abs(koff - d - qoff) <= WINDOW, 0.0, -np.inf) for d in (0, WINDOW, 2 * WINDOW)]
                    ).astype(np.float32)


def _attn_b_kernel(sink_ref, mask_ref, q_ref, k_ref, v_ref, o_ref, *, seq):
    j = pl.program_id(1)
    t = pl.program_id(2)
    qb = Q_BLOCK
    g = B_HEADS // B_KV_HEADS
    band = qb + 2 * WINDOW
    nsub = q_ref.shape[0] // qb
    hrow = lax.broadcasted_iota(jnp.int32, (g * qb, 1), 0) // qb
    sink = jnp.zeros((g * qb, 1), _F32)
    for h in range(g):
        sink = jnp.where(hrow == h, sink_ref[j * g + h] * _LOG2E, sink)
    for u in range(nsub):
        blk = t * nsub + u
        q0 = blk * qb
        start = pl.multiple_of(jnp.clip(q0 - WINDOW, 0, seq - band), LANES)
        kind = jnp.where(blk == 0, 0, jnp.where(blk == seq // qb - 1, 2, 1))
        kb = k_ref[pl.ds(start, band), :]
        vb = v_ref[pl.ds(start, band), :]
        qs = _stack_heads(q_ref[u * qb:(u + 1) * qb, :])
        s = _dot_nt(qs, kb) + mask_ref[kind]
        m = jnp.maximum(jnp.max(s, axis=-1, keepdims=True), sink)
        p = jnp.exp2(s - m)
        l = jnp.sum(p, axis=-1, keepdims=True) + jnp.exp2(sink - m)
        o = _dot(p.astype(_BF16), vb) / l
        o_ref[u * qb:(u + 1) * qb, :] = _unstack_heads(o, qb).astype(o_ref.dtype)


def _attn_b(qkv, sink, *, batch, seq, tq):
    M = qkv.shape[0]
    nt = seq // tq
    assert tq % Q_BLOCK == 0 and seq // Q_BLOCK >= 2
    masks = jnp.asarray(_b_band_masks())
    gw = (B_HEADS // B_KV_HEADS) * HEAD_DIM
    base = (A_HEADS + 4 * A_KV_HEADS) * HEAD_DIM
    q_col = base // gw
    k_col = (base + B_HEADS * HEAD_DIM) // LANES
    v_col = k_col + B_KV_HEADS
    return pl.pallas_call(
        functools.partial(_attn_b_kernel, seq=seq),
        grid=(batch, B_KV_HEADS, nt),
        in_specs=[
            pl.BlockSpec(memory_space=pltpu.SMEM),
            pl.BlockSpec(masks.shape, lambda b, j, t: (0, 0, 0), pipeline_mode=pl.Buffered(1)),
            pl.BlockSpec((tq, gw), lambda b, j, t: (b * nt + t, q_col + j)),
            pl.BlockSpec((seq, LANES), lambda b, j, t: (b, k_col + j)),
            pl.BlockSpec((seq, LANES), lambda b, j, t: (b, v_col + j)),
        ],
        out_specs=pl.BlockSpec((tq, gw), lambda b, j, t: (b * nt + t, j)),
        out_shape=jax.ShapeDtypeStruct((M, B_HEADS * HEAD_DIM), _BF16),
        compiler_params=_params("parallel", "parallel", "parallel"),
        name="attn_b",
    )(sink, masks, qkv, qkv, qkv)


C_GROUP = 4
C_BAND = C_GROUP + NA_KH


def _attn_c_kernel(q_ref, k_ref, v_ref, bias_ref, o_ref, *, rows):
    nq = C_GROUP * GRID_W
    nk = C_BAND * GRID_W
    groups = rows // C_GROUP

    def band(g):
        k0 = min(max(g * C_GROUP - NA_KH // 2, 0), rows - C_BAND) * GRID_W
        return slice(k0, k0 + nk)

    for g in range(groups):
        qs = _stack_heads(q_ref[g * nq:(g + 1) * nq, :])
        s = _dot_nt(qs, k_ref[band(g), :])
        kind = 0 if g == 0 else (2 if g == groups - 1 else 1)
        o = _softmax2_pv(s + bias_ref[kind], v_ref[band(g), :])
        o_ref[g * nq:(g + 1) * nq, :] = _unstack_heads(o, nq).astype(o_ref.dtype)


def _attn_c(qkv, bias, *, batch, seq):
    M = qkv.shape[0]
    width = qkv.shape[1] // 3
    npair = width // LANES
    rows = seq // GRID_W
    return pl.pallas_call(
        functools.partial(_attn_c_kernel, rows=rows),
        grid=(batch, npair),
        in_specs=[
            pl.BlockSpec((seq, LANES), lambda b, p: (b, p)),
            pl.BlockSpec((seq, LANES), lambda b, p: (b, npair + p)),
            pl.BlockSpec((seq, LANES), lambda b, p: (b, 2 * npair + p)),
            pl.BlockSpec((None,) + bias.shape[1:], lambda b, p: (p, 0, 0, 0)),
        ],
        out_specs=pl.BlockSpec((seq, LANES), lambda b, p: (b, p)),
        out_shape=jax.ShapeDtypeStruct((M, width), _BF16),
        compiler_params=_params("parallel", "parallel"),
        name="attn_c",
    )(qkv, qkv, qkv, bias)


def _c_window(kind, u):
    if kind == 0:
        return 0, NA_KH, 0
    if kind == 1:
        return u, u + NA_KH, -(NA_KH // 2)
    return C_GROUP, C_GROUP + NA_KH, C_GROUP - C_BAND


def _c_bias_kernel(rpb_ref, o_ref):
    nq = C_GROUP * GRID_W
    w = lax.broadcasted_iota(jnp.int32, (GRID_W, LANES), 0)
    lane = lax.broadcasted_iota(jnp.int32, (GRID_W, LANES), 1)
    kc = jnp.bitwise_and(lane, GRID_W - 1)
    upper = lane >= GRID_W
    cs = jnp.clip(w - NA_KW // 2, 0, GRID_W - NA_KW)
    inside = (kc >= cs) & (kc < cs + NA_KW)
    for hh in range(2):
        lo_t, hi_t = [], []
        for i in range(2 * NA_KH - 1):
            r = jnp.broadcast_to(rpb_ref[hh, i:i + 1, :], (GRID_W, LANES))
            lo_t.append(pltpu.roll(r, LANES - (NA_KW - 1), 1, stride=1, stride_axis=0))
            hi_t.append(pltpu.roll(r, GRID_W - (NA_KW - 1), 1, stride=1, stride_axis=0))
        neg = jnp.full((GRID_W, LANES), -jnp.inf, _F32)
        for kind in range(3):
            for u in range(C_GROUP):
                lo, hi, shift = _c_window(kind, u)
                for b2 in range(C_BAND // 2):
                    halves = []
                    for b, table in ((2 * b2, lo_t), (2 * b2 + 1, hi_t)):
                        halves.append(table[shift + b - u + NA_KH - 1] if lo <= b < hi else neg)
                    tile = jnp.where(inside, jnp.where(upper, halves[1], halves[0]) * _LOG2E, neg)
                    r0 = hh * nq + u * GRID_W
                    o_ref[kind, r0:r0 + GRID_W, b2 * LANES:(b2 + 1) * LANES] = tile


def _c_bias_table(rpb):
    H, nr, nc = rpb.shape
    assert nr == 2 * NA_KH - 1 and nc == 2 * NA_KW - 1 and 2 * GRID_W == LANES and H % 2 == 0 and C_BAND % 2 == 0
    rpb_pad = jnp.pad(rpb, ((0, 0), (0, 0), (0, LANES - nc)))
    shape = (3, 2 * C_GROUP * GRID_W, C_BAND * GRID_W)
    return pl.pallas_call(
        _c_bias_kernel,
        grid=(H // 2,),
        in_specs=[pl.BlockSpec((2, nr, LANES), lambda p: (p, 0, 0))],
        out_specs=pl.BlockSpec((None,) + shape, lambda p: (p, 0, 0, 0)),
        out_shape=jax.ShapeDtypeStruct((H // 2,) + shape, _F32),
        compiler_params=_params("parallel"),
        name="c_bias",
    )(rpb_pad)


def _rope_tables(ang):
    half = HEAD_DIM // 2
    cos, sin = jnp.cos(ang), jnp.sin(ang)
    zero = jnp.zeros_like(sin)
    cos_h = jnp.concatenate([cos, cos], axis=-1)
    sin_lo = jnp.concatenate([-sin, zero], axis=-1)
    sin_hi = jnp.concatenate([zero, sin], axis=-1)
    assert cos_h.shape[-1] == 2 * half
    return tuple(jnp.concatenate([t, t], axis=-1).astype(_F32) for t in (cos_h, sin_lo, sin_hi))


def _rope_angles(pos, dim):
    inv = ROPE_THETA ** (-jnp.arange(0, dim, 2, dtype=_F32) / dim)
    return pos[:, None] * inv[None, :]


def _ab_columns():
    d = HEAD_DIM
    cols, c = [], 0
    for n_q, n_kv in ((A_HEADS, A_KV_HEADS), (B_HEADS, B_KV_HEADS)):
        cols += list(range(c, c + n_q * d))
        c += n_q * d
        for _ in range(2):
            for h in range(n_kv):
                cols += 2 * list(range(c + h * d, c + (h + 1) * d))
            c += n_kv * d
    return np.asarray(cols, dtype=np.int32)


def _pick(n, pref):
    t = min(n, pref)
    assert n % t == 0, (n, pref)
    return t


def _tiles(batch, seq, d_ff, mem_tokens):
    m = batch * seq
    return dict(
        ffn_rows=_pick(m, 512),
        ffn_cols=_pick(d_ff, 2 * LANES),
        proj_rows=_pick(m, 1024),
        proj_cols=2 * 2 * LANES,
        seq_rows=_pick(seq, 1024),
        mem_rows=_pick(batch * mem_tokens, 512),
        a_queries=_pick(seq, 512), a_keys_bounded=_pick(seq, 256), a_keys_running_max=_pick(seq, 512),
        b_queries=_pick(seq, 8 * Q_BLOCK),
    )


def kernel(x, mem, ln_g, ln_b, ffn_w_gate, ffn_w_up, ffn_w_down, ab_w_in, ab_w_out, ab_q_gain, ab_k_gain, ab_sink,
           c_w_in, c_w_out, c_rpb, mem_w_q, mem_w_kv, mem_w_o):
    B, S, D = x.shape
    depth = ln_g.shape[0]
    M = B * S
    MT = mem.shape[1]
    rows = S // GRID_W
    assert S % GRID_W == 0 and rows >= C_BAND and rows % C_GROUP == 0 and NA_KH // 2 == C_GROUP
    assert S % Q_BLOCK == 0 and S >= Q_BLOCK + 2 * WINDOW
    assert D % (MEM_HEADS * LANES) == 0 and D == 2 * A_HEADS * HEAD_DIM
    alpha = (2.0 * depth) ** 0.25

    t = jnp.arange(S)
    row = (t // GRID_W).astype(_F32)
    colp = (t % GRID_W).astype(_F32)
    ang_2d = jnp.concatenate([_rope_angles(row, HEAD_DIM // 2), _rope_angles(colp, HEAD_DIM // 2)], axis=-1)
    ang_1d = _rope_angles(t.astype(_F32), HEAD_DIM)
    tabs2 = _rope_tables(ang_2d)
    tabs1 = _rope_tables(ang_1d)
    lane_head = np.arange(LANES) // HEAD_DIM
    bd = jnp.asarray((lane_head[:, None] == lane_head[None, :]) / HEAD_DIM, dtype=_BF16)

    wg = ffn_w_gate.astype(_BF16)
    wu = ffn_w_up.astype(_BF16)
    wd = ffn_w_down.astype(_BF16)
    ab_cols = _ab_columns()

    tiles = _tiles(B, S, wg.shape[-1], MT)
    ffn = functools.partial(_ffn_ln, alpha=alpha, tm=tiles["ffn_rows"], tf=tiles["ffn_cols"])
    proj = functools.partial(_proj, tn=tiles["proj_cols"])

    xf = x.reshape(M, D)
    memf = mem.reshape(B * MT, D)
    for i in range(depth):
        j = i // 2
        g = lambda k: ln_g[i, k].reshape(1, D)
        bb = lambda k: ln_b[i, k].reshape(1, D)
        xf = ffn(xf, wg, wu, wd, g(0), bb(0), i, 0)
        if i % 2 == 0:
            w_ext = ab_w_in[j][:, ab_cols].astype(_BF16)
            qg = jnp.tile(ab_q_gain[j], LANES // HEAD_DIM).reshape(1, LANES)
            kg = jnp.tile(ab_k_gain[j], LANES // HEAD_DIM).reshape(1, LANES)
            qkv = _ab_proj(xf, w_ext, bd, qg, kg, tabs2, tabs1, seq=S, tm=tiles["seq_rows"])
            attn_a = functools.partial(_attn_a, batch=B, seq=S, tq=tiles["a_queries"])
            out_a = lax.cond(_a_score_bound(ab_q_gain[j], ab_k_gain[j]) <= A_SCORE_BOUND,
                             functools.partial(attn_a, kc=tiles["a_keys_bounded"], bounded=True),
                             functools.partial(attn_a, kc=tiles["a_keys_running_max"], bounded=False), qkv)
            out_b = _attn_b(qkv, ab_sink[j], batch=B, seq=S, tq=tiles["b_queries"])
            w_out = ab_w_out[j].astype(_BF16)
            na = A_HEADS * HEAD_DIM
            acts, ws = [out_a, out_b], [w_out[:na], w_out[na:]]
        else:
            width = c_w_in.shape[-1] // 3
            qkv = proj(xf, c_w_in[j].astype(_BF16), tm=tiles["proj_rows"], scaled=width,
                       scale=HEAD_DIM ** -0.5 * _LOG2E)
            acts, ws = [_attn_c(qkv, _c_bias_table(c_rpb[j]), batch=B, seq=S)], [c_w_out[j].astype(_BF16)]
        kv = proj(memf, mem_w_kv[i].astype(_BF16), tm=tiles["mem_rows"])
        xf = _mix_mem(acts, ws, xf, g(1), bb(1), mem_w_q[i].astype(_BF16), kv, mem_w_o[i].astype(_BF16), g(2), bb(2),
                      alpha=alpha, batch=B, seq=S, tm=tiles["seq_rows"])
        xf = ffn(xf, wg, wu, wd, g(3), bb(3), i, 1)
    return xf.reshape(B, S, D)
```

```python
import functools

import numpy as np
import jax
import jax.numpy as jnp
from jax import lax
from jax.experimental import pallas as pl
from jax.experimental.pallas import tpu as pltpu

GRID_W = 64
HEAD_DIM = 64
A_HEADS = 8
A_KV_HEADS = 2
B_HEADS = 8
B_KV_HEADS = 2
Q_BLOCK = 128
WINDOW = 128
NA_KH = 8
NA_KW = 16
MEM_HEADS = 4
ROPE_THETA = 10000.0
LN_EPS = 1e-5
RMS_EPS = 1e-6

LANES = 128
VMEM_LIMIT_BYTES = 48 * 1024 * 1024

_BF16 = jnp.bfloat16
_F32 = jnp.float32
_NT = (((1,), (1,)), ((), ()))
_LOG2E = 1.4426950408889634


def _dot(a, b):
    return jnp.dot(a, b, preferred_element_type=_F32)


def _dot_nt(a, b):
    return lax.dot_general(a, b, _NT, preferred_element_type=_F32)


def _params(*sem):
    return pltpu.CompilerParams(dimension_semantics=sem, vmem_limit_bytes=VMEM_LIMIT_BYTES)


def _layer_norm(z, g, b):
    mu = jnp.mean(z, axis=-1, keepdims=True)
    zc = z - mu
    var = jnp.mean(zc * zc, axis=-1, keepdims=True)
    return zc * lax.rsqrt(var + LN_EPS) * g + b


_FFN_OUT_SLABS = 2


def _ffn_ln_kernel(x_ref, wg_ref, wu_ref, wd_ref, g_ref, b_ref, o_ref, act_ref, *, alpha, tf):
    xb = x_ref[...].astype(_BF16)
    for c in range(0, wg_ref.shape[1], tf):
        gate = _dot(xb, wg_ref[:, c:c + tf])
        up = _dot(xb, wu_ref[:, c:c + tf])
        act_ref[:, c:c + tf] = (gate * jax.nn.sigmoid(gate) * up).astype(_BF16)
    slab = x_ref.shape[0] // _FFN_OUT_SLABS
    for r in range(0, x_ref.shape[0], slab):
        y = _dot(act_ref[r:r + slab, :], wd_ref[...])
        o_ref[r:r + slab, :] = _layer_norm(alpha * x_ref[r:r + slab, :] + 0.5 * y, g_ref[...], b_ref[...])


def _ffn_ln(x, wg, wu, wd, g, b, layer, half, *, alpha, tm, tf):
    M, D = x.shape
    F = wg.shape[-1]
    resident = pl.Buffered(1)
    return pl.pallas_call(
        functools.partial(_ffn_ln_kernel, alpha=alpha, tf=tf),
        grid=(M // tm,),
        in_specs=[
            pl.BlockSpec((tm, D), lambda i: (i, 0)),
            pl.BlockSpec((None, None, D, F), lambda i: (layer, half, 0, 0), pipeline_mode=resident),
            pl.BlockSpec((None, None, D, F), lambda i: (layer, half, 0, 0), pipeline_mode=resident),
            pl.BlockSpec((None, None, F, D), lambda i: (layer, half, 0, 0), pipeline_mode=resident),
            pl.BlockSpec((1, D), lambda i: (0, 0)),
            pl.BlockSpec((1, D), lambda i: (0, 0)),
        ],
        out_specs=pl.BlockSpec((tm, D), lambda i: (i, 0)),
        out_shape=jax.ShapeDtypeStruct((M, D), _F32),
        scratch_shapes=[pltpu.VMEM((tm, F), _BF16)],
        compiler_params=_params("parallel"),
        name="ffn_ln",
    )(x, wg, wu, wd, g, b)


def _proj_kernel(x_ref, w_ref, o_ref, *, tn, scaled, scale):
    xb = x_ref[...].astype(_BF16)
    for c in range(0, w_ref.shape[1], tn):
        y = _dot(xb, w_ref[:, c:c + tn])
        if c < scaled:
            y = y * scale
        o_ref[:, c:c + tn] = y.astype(o_ref.dtype)


def _proj(x, w, *, tm, tn, scaled=0, scale=1.0):
    M, K = x.shape
    N = w.shape[1]
    assert scaled % tn == 0
    return pl.pallas_call(
        functools.partial(_proj_kernel, tn=tn, scaled=scaled, scale=scale),
        grid=(M // tm,),
        in_specs=[pl.BlockSpec((tm, K), lambda i: (i, 0)),
                  pl.BlockSpec((K, N), lambda i: (0, 0), pipeline_mode=pl.Buffered(1))],
        out_specs=pl.BlockSpec((tm, N), lambda i: (i, 0)),
        out_shape=jax.ShapeDtypeStruct((M, N), _BF16),
        compiler_params=_params("parallel"),
        name="proj",
    )(x, w)


def _softmax2_pv(s, v):
    m = jnp.max(s, axis=-1, keepdims=True)
    p = jnp.exp2(s - m)
    l = jnp.sum(p, axis=-1, keepdims=True)
    return _dot(p.astype(_BF16), v) / l


_MIX_SLABS = 4


def _mix_mem_kernel(*refs, alpha, n_in, scale):
    a_refs = refs[:n_in]
    w_refs = refs[n_in:2 * n_in]
    x_ref, g1_ref, b1_ref, wq_ref, k_ref, v_ref, wo_ref, g2_ref, b2_ref, o_ref = refs[2 * n_in:]
    hd = x_ref.shape[1] // MEM_HEADS
    step = x_ref.shape[0] // _MIX_SLABS
    slabs = [slice(r, r + step) for r in range(0, x_ref.shape[0], step)]
    ys = []
    for rows in slabs:
        y = _dot(a_refs[0][rows, :], w_refs[0][...])
        for a_ref, w_ref in zip(a_refs[1:], w_refs[1:]):
            y = y + _dot(a_ref[rows, :], w_ref[...])
        ys.append(y)
    x1s = [_layer_norm(alpha * x_ref[rows, :] + y, g1_ref[...], b1_ref[...]) for rows, y in zip(slabs, ys)]
    qs = [(_dot(x1.astype(_BF16), wq_ref[...]) * scale).astype(_BF16) for x1 in x1s]
    os = []
    for q in qs:
        heads = []
        for h in range(MEM_HEADS):
            sl = slice(h * hd, (h + 1) * hd)
            heads.append(_softmax2_pv(_dot_nt(q[:, sl], k_ref[:, sl]), v_ref[:, sl]).astype(_BF16))
        os.append(jnp.concatenate(heads, axis=1))
    y2s = [_dot(o, wo_ref[...]) for o in os]
    for rows, x1, y2 in zip(slabs, x1s, y2s):
        o_ref[rows, :] = _layer_norm(alpha * x1 + y2, g2_ref[...], b2_ref[...])


def _mix_mem(acts, ws, x, g1, b1, wq, kv, wo, g2, b2, *, alpha, batch, seq, tm):
    M, D = x.shape
    n_in = len(acts)
    mt = kv.shape[0] // batch
    nt = seq // tm
    row = lambda b, t: (b * nt + t, 0)
    const = lambda b, t: (0, 0)
    resident = pl.Buffered(1)
    in_specs = [pl.BlockSpec((tm, a.shape[1]), row) for a in acts]
    in_specs += [pl.BlockSpec(w.shape, const, pipeline_mode=resident) for w in ws]
    in_specs += [
        pl.BlockSpec((tm, D), row), pl.BlockSpec((1, D), const), pl.BlockSpec((1, D), const),
        pl.BlockSpec((D, D), const, pipeline_mode=resident),
        pl.BlockSpec((mt, D), lambda b, t: (b, 0)),
        pl.BlockSpec((mt, D), lambda b, t: (b, 1)),
        pl.BlockSpec((D, D), const, pipeline_mode=resident),
        pl.BlockSpec((1, D), const), pl.BlockSpec((1, D), const),
    ]
    return pl.pallas_call(
        functools.partial(_mix_mem_kernel, alpha=alpha, n_in=n_in, scale=(D // MEM_HEADS) ** -0.5 * _LOG2E),
        grid=(batch, nt),
        in_specs=in_specs,
        out_specs=pl.BlockSpec((tm, D), row),
        out_shape=jax.ShapeDtypeStruct((M, D), _F32),
        compiler_params=_params("parallel", "parallel"),
        name="mix_mem",
    )(*acts, *ws, x, g1, b1, wq, kv, kv, wo, g2, b2)


_AB_CHUNK = 256
_AB_LAYOUT = (("qa", 512), ("ka", 256), ("va", 256), ("qb", 512), ("kb", 256), ("vb", 256))


def _rope(y, cos, sin_lo, sin_hi):
    return y * cos + pltpu.roll(y, 96, axis=1) * sin_lo + pltpu.roll(y, 32, axis=1) * sin_hi


def _head_mean_square(y, bd):
    sq = y * y
    hi = sq.astype(_BF16)
    lo = (sq - hi.astype(_F32)).astype(_BF16)
    return _dot(hi, bd) + _dot(lo, bd)


def _ab_proj_kernel(x_ref, w_ref, bd_ref, qg_ref, kg_ref, c2_ref, sl2_ref, sh2_ref, c1_ref, sl1_ref, sh1_ref,
                    o_ref, *, scale):
    xb = x_ref[...].astype(_BF16)
    bd = bd_ref[...]
    chunks, col = [], 0
    for name, width in _AB_LAYOUT:
        chunks += [(name, c) for c in range(col, col + width, _AB_CHUNK)]
        col += width
    normed = [nc for nc in chunks if nc[0] in ("qa", "ka")]
    plain = [nc for nc in chunks if nc[0] not in ("qa", "ka")]
    groups = range(0, _AB_CHUNK, LANES)

    def finish(name, c, h, y):
        if name in ("qa", "ka"):
            y = _rope(y, c2_ref[...], sl2_ref[...], sh2_ref[...])
        elif name in ("qb", "kb"):
            y = _rope(y, c1_ref[...], sl1_ref[...], sh1_ref[...])
        if name in ("qa", "qb"):
            y = y * (scale * _LOG2E)
        o_ref[:, c + h:c + h + LANES] = y.astype(o_ref.dtype)

    raw = {c: _dot(xb, w_ref[:, c:c + _AB_CHUNK]) for _, c in normed}
    ms = {(c, h): _head_mean_square(raw[c][:, h:h + LANES], bd) for _, c in normed for h in groups}
    for name, c in plain:
        y2 = _dot(xb, w_ref[:, c:c + _AB_CHUNK])
        for h in groups:
            finish(name, c, h, y2[:, h:h + LANES])
    for name, c in normed:
        gain = qg_ref[...] if name == "qa" else kg_ref[...]
        for h in groups:
            finish(name, c, h, raw[c][:, h:h + LANES] * lax.rsqrt(ms[(c, h)] + RMS_EPS) * gain)


def _ab_proj(x, w_ext, bd, qg, kg, tabs2, tabs1, *, seq, tm):
    M, D = x.shape
    N = w_ext.shape[1]
    nt = seq // tm
    tab_spec = pl.BlockSpec((tm, LANES), lambda i: (i % nt, 0))
    const = lambda shape: pl.BlockSpec(shape, lambda i: (0, 0))
    return pl.pallas_call(
        functools.partial(_ab_proj_kernel, scale=HEAD_DIM ** -0.5),
        grid=(M // tm,),
        in_specs=[pl.BlockSpec((tm, D), lambda i: (i, 0)), const((D, N)), const((LANES, LANES)),
                  const((1, LANES)), const((1, LANES))] + [tab_spec] * 6,
        out_specs=pl.BlockSpec((tm, N), lambda i: (i, 0)),
        out_shape=jax.ShapeDtypeStruct((M, N), _BF16),
        compiler_params=_params("parallel"),
        name="ab_proj",
    )(x, w_ext, bd, qg, kg, *tabs2, *tabs1)


def _stack_heads(q):
    t = q.shape[0]
    lo = lax.broadcasted_iota(jnp.int32, (t, LANES), 1) < HEAD_DIM
    zero = jnp.zeros((t, LANES), q.dtype)
    parts = []
    for c in range(0, q.shape[1], LANES):
        pair = q[:, c:c + LANES]
        parts += [jnp.where(lo, pair, zero), jnp.where(lo, zero, pair)]
    return jnp.concatenate(parts, axis=0)


def _unstack_heads(o, t):
    lo = lax.broadcasted_iota(jnp.int32, (t, LANES), 1) < HEAD_DIM
    n = o.shape[0] // t
    return jnp.concatenate(
        [jnp.where(lo, o[(2 * i) * t:(2 * i + 1) * t], o[(2 * i + 1) * t:(2 * i + 2) * t]) for i in range(n // 2)],
        axis=1)


def _attn_a_kernel(q_ref, k_ref, v_ref, o_ref, *, kc):
    tq = q_ref.shape[0]
    qs = _stack_heads(q_ref[...])
    m = l = acc = None
    for c in range(0, k_ref.shape[0], kc):
        s = _dot_nt(qs, k_ref[c:c + kc, :])
        mc = jnp.max(s, axis=-1, keepdims=True)
        m_new = mc if m is None else jnp.maximum(m, mc)
        p = jnp.exp2(s - m_new)
        lc = p[:, :LANES]
        for i in range(LANES, kc, LANES):
            lc = lc + p[:, i:i + LANES]
        pv = _dot(p.astype(_BF16), v_ref[c:c + kc, :])
        if m is None:
            l, acc = lc, pv
        else:
            corr = jnp.exp2(m - m_new)
            l, acc = l * corr + lc, acc * corr + pv
        m = m_new
    o_ref[...] = _unstack_heads(acc / jnp.sum(l, axis=-1, keepdims=True), tq).astype(o_ref.dtype)


def _attn_a_bounded_kernel(q_ref, k_ref, v_ref, o_ref, *, kc):
    tq = q_ref.shape[0]
    qs = _stack_heads(q_ref[...])
    l = acc = None
    for c in range(0, k_ref.shape[0], kc):
        p = jnp.exp2(_dot_nt(qs, k_ref[c:c + kc, :]))
        lc = p[:, :LANES]
        for i in range(LANES, kc, LANES):
            lc = lc + p[:, i:i + LANES]
        pv = _dot(p.astype(_BF16), v_ref[c:c + kc, :])
        l, acc = (lc, pv) if l is None else (l + lc, acc + pv)
    o_ref[...] = _unstack_heads(acc / jnp.sum(l, axis=-1, keepdims=True), tq).astype(o_ref.dtype)


A_SCORE_BOUND = 60.0


def _a_score_bound(q_gain, k_gain):
    return 1.02 * HEAD_DIM ** 0.5 * _LOG2E * jnp.max(jnp.abs(q_gain)) * jnp.max(jnp.abs(k_gain))


def _attn_a(qkv, *, batch, seq, tq, kc, bounded):
    M = qkv.shape[0]
    nt = seq // tq
    gw = (A_HEADS // A_KV_HEADS) * HEAD_DIM
    k_col = A_HEADS * HEAD_DIM // LANES
    v_col = k_col + A_KV_HEADS
    return pl.pallas_call(
        functools.partial(_attn_a_bounded_kernel if bounded else _attn_a_kernel, kc=kc),
        grid=(batch, A_KV_HEADS, nt),
        in_specs=[
            pl.BlockSpec((tq, gw), lambda b, j, t: (b * nt + t, j)),
            pl.BlockSpec((seq, LANES), lambda b, j, t: (b, k_col + j)),
            pl.BlockSpec((seq, LANES), lambda b, j, t: (b, v_col + j)),
        ],
        out_specs=pl.BlockSpec((tq, gw), lambda b, j, t: (b * nt + t, j)),
        out_shape=jax.ShapeDtypeStruct((M, A_HEADS * HEAD_DIM), _BF16),
        compiler_params=_params("parallel", "parallel", "parallel"),
        name="attn_a",
    )(qkv, qkv, qkv)


def _b_band_masks():
    g = B_HEADS // B_KV_HEADS
    qoff = (np.arange(g * Q_BLOCK) % Q_BLOCK)[:, None]
    koff = np.arange(Q_BLOCK + 2 * WINDOW)[None, :]
    return np.stack([np.where(np.abs(koff - d - qoff) <= WINDOW, 0.0, -np.inf) for d in (0, WINDOW, 2 * WINDOW)]
                    ).astype(np.float32)


def _attn_b_kernel(sink_ref, mask_ref, q_ref, k_ref, v_ref, o_ref, *, seq):
    j = pl.program_id(1)
    t = pl.program_id(2)
    qb = Q_BLOCK
    g = B_HEADS // B_KV_HEADS
    band = qb + 2 * WINDOW
    nsub = q_ref.shape[0] // qb
    hrow = lax.broadcasted_iota(jnp.int32, (g * qb, 1), 0) // qb
    sink = jnp.zeros((g * qb, 1), _F32)
    for h in range(g):
        sink = jnp.where(hrow == h, sink_ref[j * g + h] * _LOG2E, sink)
    for u in range(nsub):
        blk = t * nsub + u
        q0 = blk * qb
        start = pl.multiple_of(jnp.clip(q0 - WINDOW, 0, seq - band), LANES)
        kind = jnp.where(blk == 0, 0, jnp.where(blk == seq // qb - 1, 2, 1))
        kb = k_ref[pl.ds(start, band), :]
        vb = v_ref[pl.ds(start, band), :]
        qs = _stack_heads(q_ref[u * qb:(u + 1) * qb, :])
        s = _dot_nt(qs, kb) + mask_ref[kind]
        m = jnp.maximum(jnp.max(s, axis=-1, keepdims=True), sink)
        p = jnp.exp2(s - m)
        l = jnp.sum(p, axis=-1, keepdims=True) + jnp.exp2(sink - m)
        o = _dot(p.astype(_BF16), vb) / l
        o_ref[u * qb:(u + 1) * qb, :] = _unstack_heads(o, qb).astype(o_ref.dtype)


def _attn_b(qkv, sink, *, batch, seq, tq):
    M = qkv.shape[0]
    nt = seq // tq
    assert tq % Q_BLOCK == 0 and seq // Q_BLOCK >= 2
    masks = jnp.asarray(_b_band_masks())
    gw = (B_HEADS // B_KV_HEADS) * HEAD_DIM
    base = (A_HEADS + 4 * A_KV_HEADS) * HEAD_DIM
    q_col = base // gw
    k_col = (base + B_HEADS * HEAD_DIM) // LANES
    v_col = k_col + B_KV_HEADS
    return pl.pallas_call(
        functools.partial(_attn_b_kernel, seq=seq),
        grid=(batch, B_KV_HEADS, nt),
        in_specs=[
            pl.BlockSpec(memory_space=pltpu.SMEM),
            pl.BlockSpec(masks.shape, lambda b, j, t: (0, 0, 0), pipeline_mode=pl.Buffered(1)),
            pl.BlockSpec((tq, gw), lambda b, j, t: (b * nt + t, q_col + j)),
            pl.BlockSpec((seq, LANES), lambda b, j, t: (b, k_col + j)),
            pl.BlockSpec((seq, LANES), lambda b, j, t: (b, v_col + j)),
        ],
        out_specs=pl.BlockSpec((tq, gw), lambda b, j, t: (b * nt + t, j)),
        out_shape=jax.ShapeDtypeStruct((M, B_HEADS * HEAD_DIM), _BF16),
        compiler_params=_params("parallel", "parallel", "parallel"),
        name="attn_b",
    )(sink, masks, qkv, qkv, qkv)


C_GROUP = 4
C_BAND = C_GROUP + NA_KH


def _attn_c_kernel(q_ref, k_ref, v_ref, bias_ref, o_ref, *, rows):
    nq = C_GROUP * GRID_W
    nk = C_BAND * GRID_W
    groups = rows // C_GROUP

    def band(g):
        k0 = min(max(g * C_GROUP - NA_KH // 2, 0), rows - C_BAND) * GRID_W
        return slice(k0, k0 + nk)

    for g in range(groups):
        qs = _stack_heads(q_ref[g * nq:(g + 1) * nq, :])
        s = _dot_nt(qs, k_ref[band(g), :])
        kind = 0 if g == 0 else (2 if g == groups - 1 else 1)
        o = _softmax2_pv(s + bias_ref[kind], v_ref[band(g), :])
        o_ref[g * nq:(g + 1) * nq, :] = _unstack_heads(o, nq).astype(o_ref.dtype)


def _attn_c(qkv, bias, *, batch, seq):
    M = qkv.shape[0]
    width = qkv.shape[1] // 3
    npair = width // LANES
    rows = seq // GRID_W
    return pl.pallas_call(
        functools.partial(_attn_c_kernel, rows=rows),
        grid=(batch, npair),
        in_specs=[
            pl.BlockSpec((seq, LANES), lambda b, p: (b, p)),
            pl.BlockSpec((seq, LANES), lambda b, p: (b, npair + p)),
            pl.BlockSpec((seq, LANES), lambda b, p: (b, 2 * npair + p)),
            pl.BlockSpec((None,) + bias.shape[1:], lambda b, p: (p, 0, 0, 0)),
        ],
        out_specs=pl.BlockSpec((seq, LANES), lambda b, p: (b, p)),
        out_shape=jax.ShapeDtypeStruct((M, width), _BF16),
        compiler_params=_params("parallel", "parallel"),
        name="attn_c",
    )(qkv, qkv, qkv, bias)


def _c_window(kind, u):
    if kind == 0:
        return 0, NA_KH, 0
    if kind == 1:
        return u, u + NA_KH, -(NA_KH // 2)
    return C_GROUP, C_GROUP + NA_KH, C_GROUP - C_BAND


def _c_bias_kernel(rpb_ref, o_ref):
    nq = C_GROUP * GRID_W
    w = lax.broadcasted_iota(jnp.int32, (GRID_W, LANES), 0)
    lane = lax.broadcasted_iota(jnp.int32, (GRID_W, LANES), 1)
    kc = jnp.bitwise_and(lane, GRID_W - 1)
    upper = lane >= GRID_W
    cs = jnp.clip(w - NA_KW // 2, 0, GRID_W - NA_KW)
    inside = (kc >= cs) & (kc < cs + NA_KW)
    for hh in range(2):
        lo_t, hi_t = [], []
        for i in range(2 * NA_KH - 1):
            r = jnp.broadcast_to(rpb_ref[hh, i:i + 1, :], (GRID_W, LANES))
            lo_t.append(pltpu.roll(r, LANES - (NA_KW - 1), 1, stride=1, stride_axis=0))
            hi_t.append(pltpu.roll(r, GRID_W - (NA_KW - 1), 1, stride=1, stride_axis=0))
        neg = jnp.full((GRID_W, LANES), -jnp.inf, _F32)
        for kind in range(3):
            for u in range(C_GROUP):
                lo, hi, shift = _c_window(kind, u)
                for b2 in range(C_BAND // 2):
                    halves = []
                    for b, table in ((2 * b2, lo_t), (2 * b2 + 1, hi_t)):
                        halves.append(table[shift + b - u + NA_KH - 1] if lo <= b < hi else neg)
                    tile = jnp.where(inside, jnp.where(upper, halves[1], halves[0]) * _LOG2E, neg)
                    r0 = hh * nq + u * GRID_W
                    o_ref[kind, r0:r0 + GRID_W, b2 * LANES:(b2 + 1) * LANES] = tile


def _c_bias_table(rpb):
    H, nr, nc = rpb.shape
    assert nr == 2 * NA_KH - 1 and nc == 2 * NA_KW - 1 and 2 * GRID_W == LANES and H % 2 == 0 and C_BAND % 2 == 0
    rpb_pad = jnp.pad(rpb, ((0, 0), (0, 0), (0, LANES - nc)))
    shape = (3, 2 * C_GROUP * GRID_W, C_BAND * GRID_W)
    return pl.pallas_call(
        _c_bias_kernel,
        grid=(H // 2,),
        in_specs=[pl.BlockSpec((2, nr, LANES), lambda p: (p, 0, 0))],
        out_specs=pl.BlockSpec((None,) + shape, lambda p: (p, 0, 0, 0)),
        out_shape=jax.ShapeDtypeStruct((H // 2,) + shape, _F32),
        compiler_params=_params("parallel"),
        name="c_bias",
    )(rpb_pad)


def _rope_tables(ang):
    half = HEAD_DIM // 2
    cos, sin = jnp.cos(ang), jnp.sin(ang)
    zero = jnp.zeros_like(sin)
    cos_h = jnp.concatenate([cos, cos], axis=-1)
    sin_lo = jnp.concatenate([-sin, zero], axis=-1)
    sin_hi = jnp.concatenate([zero, sin], axis=-1)
    assert cos_h.shape[-1] == 2 * half
    return tuple(jnp.concatenate([t, t], axis=-1).astype(_F32) for t in (cos_h, sin_lo, sin_hi))


def _rope_angles(pos, dim):
    inv = ROPE_THETA ** (-jnp.arange(0, dim, 2, dtype=_F32) / dim)
    return pos[:, None] * inv[None, :]


def _ab_columns():
    d = HEAD_DIM
    cols, c = [], 0
    for n_q, n_kv in ((A_HEADS, A_KV_HEADS), (B_HEADS, B_KV_HEADS)):
        cols += list(range(c, c + n_q * d))
        c += n_q * d
        for _ in range(2):
            for h in range(n_kv):
                cols += 2 * list(range(c + h * d, c + (h + 1) * d))
            c += n_kv * d
    return np.asarray(cols, dtype=np.int32)


def _pick(n, pref):
    t = min(n, pref)
    assert n % t == 0, (n, pref)
    return t


def _tiles(batch, seq, d_ff, mem_tokens):
    m = batch * seq
    return dict(
        ffn_rows=_pick(m, 512),
        ffn_cols=_pick(d_ff, 2 * LANES),
        proj_rows=_pick(m, 1024),
        proj_cols=2 * 2 * LANES,
        seq_rows=_pick(seq, 1024),
        mem_rows=_pick(batch * mem_tokens, 512),
        a_queries=_pick(seq, 512), a_keys_bounded=_pick(seq, 256), a_keys_running_max=_pick(seq, 512),
        b_queries=_pick(seq, 8 * Q_BLOCK),
    )


def kernel(x, mem, ln_g, ln_b, ffn_w_gate, ffn_w_up, ffn_w_down, ab_w_in, ab_w_out, ab_q_gain, ab_k_gain, ab_sink,
           c_w_in, c_w_out, c_rpb, mem_w_q, mem_w_kv, mem_w_o):
    B, S, D = x.shape
    depth = ln_g.shape[0]
    M = B * S
    MT = mem.shape[1]
    rows = S // GRID_W
    assert S % GRID_W == 0 and rows >= C_BAND and rows % C_GROUP == 0 and NA_KH // 2 == C_GROUP
    assert S % Q_BLOCK == 0 and S >= Q_BLOCK + 2 * WINDOW
    assert D % (MEM_HEADS * LANES) == 0 and D == 2 * A_HEADS * HEAD_DIM
    alpha = (2.0 * depth) ** 0.25

    t = jnp.arange(S)
    row = (t // GRID_W).astype(_F32)
    colp = (t % GRID_W).astype(_F32)
    ang_2d = jnp.concatenate([_rope_angles(row, HEAD_DIM // 2), _rope_angles(colp, HEAD_DIM // 2)], axis=-1)
    ang_1d = _rope_angles(t.astype(_F32), HEAD_DIM)
    tabs2 = _rope_tables(ang_2d)
    tabs1 = _rope_tables(ang_1d)
    lane_head = np.arange(LANES) // HEAD_DIM
    bd = jnp.asarray((lane_head[:, None] == lane_head[None, :]) / HEAD_DIM, dtype=_BF16)

    wg = ffn_w_gate.astype(_BF16)
    wu = ffn_w_up.astype(_BF16)
    wd = ffn_w_down.astype(_BF16)
    ab_cols = _ab_columns()

    tiles = _tiles(B, S, wg.shape[-1], MT)
    ffn = functools.partial(_ffn_ln, alpha=alpha, tm=tiles["ffn_rows"], tf=tiles["ffn_cols"])
    proj = functools.partial(_proj, tn=tiles["proj_cols"])

    xf = x.reshape(M, D)
    memf = mem.reshape(B * MT, D)
    for i in range(depth):
        j = i // 2
        g = lambda k: ln_g[i, k].reshape(1, D)
        bb = lambda k: ln_b[i, k].reshape(1, D)
        xf = ffn(xf, wg, wu, wd, g(0), bb(0), i, 0)
        if i % 2 == 0:
            w_ext = ab_w_in[j][:, ab_cols].astype(_BF16)
            qg = jnp.tile(ab_q_gain[j], LANES // HEAD_DIM).reshape(1, LANES)
            kg = jnp.tile(ab_k_gain[j], LANES // HEAD_DIM).reshape(1, LANES)
            qkv = _ab_proj(xf, w_ext, bd, qg, kg, tabs2, tabs1, seq=S, tm=tiles["seq_rows"])
            attn_a = functools.partial(_attn_a, batch=B, seq=S, tq=tiles["a_queries"])
            out_a = lax.cond(_a_score_bound(ab_q_gain[j], ab_k_gain[j]) <= A_SCORE_BOUND,
                             functools.partial(attn_a, kc=tiles["a_keys_bounded"], bounded=True),
                             functools.partial(attn_a, kc=tiles["a_keys_running_max"], bounded=False), qkv)
            out_b = _attn_b(qkv, ab_sink[j], batch=B, seq=S, tq=tiles["b_queries"])
            w_out = ab_w_out[j].astype(_BF16)
            na = A_HEADS * HEAD_DIM
            acts, ws = [out_a, out_b], [w_out[:na], w_out[na:]]
        else:
            width = c_w_in.shape[-1] // 3
            qkv = proj(xf, c_w_in[j].astype(_BF16), tm=tiles["proj_rows"], scaled=width,
                       scale=HEAD_DIM ** -0.5 * _LOG2E)
            acts, ws = [_attn_c(qkv, _c_bias_table(c_rpb[j]), batch=B, seq=S)], [c_w_out[j].astype(_BF16)]
        kv = proj(memf, mem_w_kv[i].astype(_BF16), tm=tiles["mem_rows"])
        xf = _mix_mem(acts, ws, xf, g(1), bb(1), mem_w_q[i].astype(_BF16), kv, mem_w_o[i].astype(_BF16), g(2), bb(2),
                      alpha=alpha, batch=B, seq=S, tm=tiles["seq_rows"])
        xf = ffn(xf, wg, wu, wd, g(3), bb(3), i, 1)
    return xf.reshape(B, S, D)
```

```python
import functools

import numpy as np
import jax
import jax.numpy as jnp
from jax import lax
from jax.experimental import pallas as pl
from jax.experimental.pallas import tpu as pltpu

GRID_W = 64
HEAD_DIM = 64
A_HEADS = 8
A_KV_HEADS = 2
B_HEADS = 8
B_KV_HEADS = 2
Q_BLOCK = 128
WINDOW = 128
NA_KH = 8
NA_KW = 16
MEM_HEADS = 4
ROPE_THETA = 10000.0
LN_EPS = 1e-5
RMS_EPS = 1e-6

LANES = 128
VMEM_LIMIT_BYTES = 48 * 1024 * 1024

_BF16 = jnp.bfloat16
_F32 = jnp.float32
_NT = (((1,), (1,)), ((), ()))
_LOG2E = 1.4426950408889634


def _dot(a, b):
    return jnp.dot(a, b, preferred_element_type=_F32)


def _dot_nt(a, b):
    return lax.dot_general(a, b, _NT, preferred_element_type=_F32)


def _params(*sem):
    return pltpu.CompilerParams(dimension_semantics=sem, vmem_limit_bytes=VMEM_LIMIT_BYTES)


def _layer_norm(z, g, b):
    mu = jnp.mean(z, axis=-1, keepdims=True)
    zc = z - mu
    var = jnp.mean(zc * zc, axis=-1, keepdims=True)
    return zc * lax.rsqrt(var + LN_EPS) * g + b


_FFN_OUT_SLABS = 2


def _ffn_ln_kernel(x_ref, wg_ref, wu_ref, wd_ref, g_ref, b_ref, o_ref, act_ref, *, alpha, tf):
    xb = x_ref[...].astype(_BF16)
    for c in range(0, wg_ref.shape[1], tf):
        gate = _dot(xb, wg_ref[:, c:c + tf])
        up = _dot(xb, wu_ref[:, c:c + tf])
        act_ref[:, c:c + tf] = (gate * jax.nn.sigmoid(gate) * up).astype(_BF16)
    slab = x_ref.shape[0] // _FFN_OUT_SLABS
    for r in range(0, x_ref.shape[0], slab):
        y = _dot(act_ref[r:r + slab, :], wd_ref[...])
        o_ref[r:r + slab, :] = _layer_norm(alpha * x_ref[r:r + slab, :] + 0.5 * y, g_ref[...], b_ref[...])


def _ffn_ln(x, wg, wu, wd, g, b, layer, half, *, alpha, tm, tf):
    M, D = x.shape
    F = wg.shape[-1]
    resident = pl.Buffered(1)
    return pl.pallas_call(
        functools.partial(_ffn_ln_kernel, alpha=alpha, tf=tf),
        grid=(M // tm,),
        in_specs=[
            pl.BlockSpec((tm, D), lambda i: (i, 0)),
            pl.BlockSpec((None, None, D, F), lambda i: (layer, half, 0, 0), pipeline_mode=resident),
            pl.BlockSpec((None, None, D, F), lambda i: (layer, half, 0, 0), pipeline_mode=resident),
            pl.BlockSpec((None, None, F, D), lambda i: (layer, half, 0, 0), pipeline_mode=resident),
            pl.BlockSpec((1, D), lambda i: (0, 0)),
            pl.BlockSpec((1, D), lambda i: (0, 0)),
        ],
        out_specs=pl.BlockSpec((tm, D), lambda i: (i, 0)),
        out_shape=jax.ShapeDtypeStruct((M, D), _F32),
        scratch_shapes=[pltpu.VMEM((tm, F), _BF16)],
        compiler_params=_params("parallel"),
        name="ffn_ln",
    )(x, wg, wu, wd, g, b)


def _proj_kernel(x_ref, w_ref, o_ref, *, tn, scaled, scale):
    xb = x_ref[...].astype(_BF16)
    for c in range(0, w_ref.shape[1], tn):
        y = _dot(xb, w_ref[:, c:c + tn])
        if c < scaled:
            y = y * scale
        o_ref[:, c:c + tn] = y.astype(o_ref.dtype)


def _proj(x, w, *, tm, tn, scaled=0, scale=1.0):
    M, K = x.shape
    N = w.shape[1]
    assert scaled % tn == 0
    return pl.pallas_call(
        functools.partial(_proj_kernel, tn=tn, scaled=scaled, scale=scale),
        grid=(M // tm,),
        in_specs=[pl.BlockSpec((tm, K), lambda i: (i, 0)),
                  pl.BlockSpec((K, N), lambda i: (0, 0), pipeline_mode=pl.Buffered(1))],
        out_specs=pl.BlockSpec((tm, N), lambda i: (i, 0)),
        out_shape=jax.ShapeDtypeStruct((M, N), _BF16),
        compiler_params=_params("parallel"),
        name="proj",
    )(x, w)


def _softmax2_pv(s, v):
    m = jnp.max(s, axis=-1, keepdims=True)
    p = jnp.exp2(s - m)
    l = jnp.sum(p, axis=-1, keepdims=True)
    return _dot(p.astype(_BF16), v) / l


_MIX_SLABS = 4


def _mix_mem_kernel(*refs, alpha, n_in, scale):
    a_refs = refs[:n_in]
    w_refs = refs[n_in:2 * n_in]
    x_ref, g1_ref, b1_ref, wq_ref, k_ref, v_ref, wo_ref, g2_ref, b2_ref, o_ref = refs[2 * n_in:]
    hd = x_ref.shape[1] // MEM_HEADS
    step = x_ref.shape[0] // _MIX_SLABS
    slabs = [slice(r, r + step) for r in range(0, x_ref.shape[0], step)]
    ys = []
    for rows in slabs:
        y = _dot(a_refs[0][rows, :], w_refs[0][...])
        for a_ref, w_ref in zip(a_refs[1:], w_refs[1:]):
            y = y + _dot(a_ref[rows, :], w_ref[...])
        ys.append(y)
    x1s = [_layer_norm(alpha * x_ref[rows, :] + y, g1_ref[...], b1_ref[...]) for rows, y in zip(slabs, ys)]
    qs = [(_dot(x1.astype(_BF16), wq_ref[...]) * scale).astype(_BF16) for x1 in x1s]
    os = []
    for q in qs:
        heads = []
        for h in range(MEM_HEADS):
            sl = slice(h * hd, (h + 1) * hd)
            heads.append(_softmax2_pv(_dot_nt(q[:, sl], k_ref[:, sl]), v_ref[:, sl]).astype(_BF16))
        os.append(jnp.concatenate(heads, axis=1))
    y2s = [_dot(o, wo_ref[...]) for o in os]
    for rows, x1, y2 in zip(slabs, x1s, y2s):
        o_ref[rows, :] = _layer_norm(alpha * x1 + y2, g2_ref[...], b2_ref[...])


def _mix_mem(acts, ws, x, g1, b1, wq, kv, wo, g2, b2, *, alpha, batch, seq, tm):
    M, D = x.shape
    n_in = len(acts)
    mt = kv.shape[0] // batch
    nt = seq // tm
    row = lambda b, t: (b * nt + t, 0)
    const = lambda b, t: (0, 0)
    resident = pl.Buffered(1)
    in_specs = [pl.BlockSpec((tm, a.shape[1]), row) for a in acts]
    in_specs += [pl.BlockSpec(w.shape, const, pipeline_mode=resident) for w in ws]
    in_specs += [
        pl.BlockSpec((tm, D), row), pl.BlockSpec((1, D), const), pl.BlockSpec((1, D), const),
        pl.BlockSpec((D, D), const, pipeline_mode=resident),
        pl.BlockSpec((mt, D), lambda b, t: (b, 0)),
        pl.BlockSpec((mt, D), lambda b, t: (b, 1)),
        pl.BlockSpec((D, D), const, pipeline_mode=resident),
        pl.BlockSpec((1, D), const), pl.BlockSpec((1, D), const),
    ]
    return pl.pallas_call(
        functools.partial(_mix_mem_kernel, alpha=alpha, n_in=n_in, scale=(D // MEM_HEADS) ** -0.5 * _LOG2E),
        grid=(batch, nt),
        in_specs=in_specs,
        out_specs=pl.BlockSpec((tm, D), row),
        out_shape=jax.ShapeDtypeStruct((M, D), _F32),
        compiler_params=_params("parallel", "parallel"),
        name="mix_mem",
    )(*acts, *ws, x, g1, b1, wq, kv, kv, wo, g2, b2)


_AB_CHUNK = 256
_AB_LAYOUT = (("qa", 512), ("ka", 256), ("va", 256), ("qb", 512), ("kb", 256), ("vb", 256))


def _rope(y, cos, sin_lo, sin_hi):
    return y * cos + pltpu.roll(y, 96, axis=1) * sin_lo + pltpu.roll(y, 32, axis=1) * sin_hi


def _head_mean_square(y, bd):
    sq = y * y
    hi = sq.astype(_BF16)
    lo = (sq - hi.astype(_F32)).astype(_BF16)
    return _dot(hi, bd) + _dot(lo, bd)


_AB_SLABS = 2


def _ab_proj_kernel(x_ref, w_ref, bd_ref, qg_ref, kg_ref, c2_ref, sl2_ref, sh2_ref, c1_ref, sl1_ref, sh1_ref,
                    o_ref, *, scale):
    bd = bd_ref[...]
    chunks, col = [], 0
    for name, width in _AB_LAYOUT:
        chunks += [(name, c) for c in range(col, col + width, _AB_CHUNK)]
        col += width
    normed = [nc for nc in chunks if nc[0] in ("qa", "ka")]
    plain = [nc for nc in chunks if nc[0] not in ("qa", "ka")]
    groups = range(0, _AB_CHUNK, LANES)
    step = x_ref.shape[0] // _AB_SLABS
    slabs = [slice(r, r + step) for r in range(0, x_ref.shape[0], step)]
    xbs = [x_ref[rows, :].astype(_BF16) for rows in slabs]

    def finish(rows, name, c, h, y):
        if name in ("qa", "ka"):
            y = _rope(y, c2_ref[rows, :], sl2_ref[rows, :], sh2_ref[rows, :])
        elif name in ("qb", "kb"):
            y = _rope(y, c1_ref[rows, :], sl1_ref[rows, :], sh1_ref[rows, :])
        if name in ("qa", "qb"):
            y = y * (scale * _LOG2E)
        o_ref[rows, c + h:c + h + LANES] = y.astype(o_ref.dtype)

    raw = [{c: _dot(xb, w_ref[:, c:c + _AB_CHUNK]) for _, c in normed} for xb in xbs]
    ms = [{(c, h): _head_mean_square(r[c][:, h:h + LANES], bd) for _, c in normed for h in groups} for r in raw]
    for name, c in plain:
        for rows, xb in zip(slabs, xbs):
            y2 = _dot(xb, w_ref[:, c:c + _AB_CHUNK])
            for h in groups:
                finish(rows, name, c, h, y2[:, h:h + LANES])
    for name, c in normed:
        gain = qg_ref[...] if name == "qa" else kg_ref[...]
        for rows, r, m in zip(slabs, raw, ms):
            for h in groups:
                finish(rows, name, c, h, r[c][:, h:h + LANES] * lax.rsqrt(m[(c, h)] + RMS_EPS) * gain)


def _ab_proj(x, w_ext, bd, qg, kg, tabs2, tabs1, *, seq, tm):
    M, D = x.shape
    N = w_ext.shape[1]
    nt = seq // tm
    tab_spec = pl.BlockSpec((tm, LANES), lambda i: (i % nt, 0))
    const = lambda shape: pl.BlockSpec(shape, lambda i: (0, 0))
    return pl.pallas_call(
        functools.partial(_ab_proj_kernel, scale=HEAD_DIM ** -0.5),
        grid=(M // tm,),
        in_specs=[pl.BlockSpec((tm, D), lambda i: (i, 0)), const((D, N)), const((LANES, LANES)),
                  const((1, LANES)), const((1, LANES))] + [tab_spec] * 6,
        out_specs=pl.BlockSpec((tm, N), lambda i: (i, 0)),
        out_shape=jax.ShapeDtypeStruct((M, N), _BF16),
        compiler_params=_params("parallel"),
        name="ab_proj",
    )(x, w_ext, bd, qg, kg, *tabs2, *tabs1)


def _stack_heads(q):
    t = q.shape[0]
    lo = lax.broadcasted_iota(jnp.int32, (t, LANES), 1) < HEAD_DIM
    zero = jnp.zeros((t, LANES), q.dtype)
    parts = []
    for c in range(0, q.shape[1], LANES):
        pair = q[:, c:c + LANES]
        parts += [jnp.where(lo, pair, zero), jnp.where(lo, zero, pair)]
    return jnp.concatenate(parts, axis=0)


def _unstack_heads(o, t):
    lo = lax.broadcasted_iota(jnp.int32, (t, LANES), 1) < HEAD_DIM
    n = o.shape[0] // t
    return jnp.concatenate(
        [jnp.where(lo, o[(2 * i) * t:(2 * i + 1) * t], o[(2 * i + 1) * t:(2 * i + 2) * t]) for i in range(n // 2)],
        axis=1)


def _attn_a_kernel(q_ref, k_ref, v_ref, o_ref, *, kc):
    tq = q_ref.shape[0]
    qs = _stack_heads(q_ref[...])
    m = l = acc = None
    for c in range(0, k_ref.shape[0], kc):
        s = _dot_nt(qs, k_ref[c:c + kc, :])
        mc = jnp.max(s, axis=-1, keepdims=True)
        m_new = mc if m is None else jnp.maximum(m, mc)
        p = jnp.exp2(s - m_new)
        lc = p[:, :LANES]
        for i in range(LANES, kc, LANES):
            lc = lc + p[:, i:i + LANES]
        pv = _dot(p.astype(_BF16), v_ref[c:c + kc, :])
        if m is None:
            l, acc = lc, pv
        else:
            corr = jnp.exp2(m - m_new)
            l, acc = l * corr + lc, acc * corr + pv
        m = m_new
    o_ref[...] = _unstack_heads(acc / jnp.sum(l, axis=-1, keepdims=True), tq).astype(o_ref.dtype)


_A_SLABS = 2


def _attn_a_bounded_kernel(q_ref, k_ref, v_ref, o_ref, *, kc):
    ts = q_ref.shape[0] // _A_SLABS
    slabs = [slice(r, r + ts) for r in range(0, q_ref.shape[0], ts)]
    qss = [_stack_heads(q_ref[rows, :]) for rows in slabs]
    ls = [None] * len(slabs)
    accs = [None] * len(slabs)
    for c in range(0, k_ref.shape[0], kc):
        for i, qs in enumerate(qss):
            p = jnp.exp2(_dot_nt(qs, k_ref[c:c + kc, :]))
            lc = p[:, :LANES]
            for j in range(LANES, kc, LANES):
                lc = lc + p[:, j:j + LANES]
            pv = _dot(p.astype(_BF16), v_ref[c:c + kc, :])
            ls[i], accs[i] = (lc, pv) if ls[i] is None else (ls[i] + lc, accs[i] + pv)
    for rows, l, acc in zip(slabs, ls, accs):
        o_ref[rows, :] = _unstack_heads(acc / jnp.sum(l, axis=-1, keepdims=True), ts).astype(o_ref.dtype)


A_SCORE_BOUND = 60.0


def _a_score_bound(q_gain, k_gain):
    return 1.02 * HEAD_DIM ** 0.5 * _LOG2E * jnp.max(jnp.abs(q_gain)) * jnp.max(jnp.abs(k_gain))


def _attn_a(qkv, *, batch, seq, tq, kc, bounded):
    M = qkv.shape[0]
    nt = seq // tq
    gw = (A_HEADS // A_KV_HEADS) * HEAD_DIM
    k_col = A_HEADS * HEAD_DIM // LANES
    v_col = k_col + A_KV_HEADS
    return pl.pallas_call(
        functools.partial(_attn_a_bounded_kernel if bounded else _attn_a_kernel, kc=kc),
        grid=(batch, A_KV_HEADS, nt),
        in_specs=[
            pl.BlockSpec((tq, gw), lambda b, j, t: (b * nt + t, j)),
            pl.BlockSpec((seq, LANES), lambda b, j, t: (b, k_col + j)),
            pl.BlockSpec((seq, LANES), lambda b, j, t: (b, v_col + j)),
        ],
        out_specs=pl.BlockSpec((tq, gw), lambda b, j, t: (b * nt + t, j)),
        out_shape=jax.ShapeDtypeStruct((M, A_HEADS * HEAD_DIM), _BF16),
        compiler_params=_params("parallel", "parallel", "parallel"),
        name="attn_a",
    )(qkv, qkv, qkv)


def _b_band_masks():
    g = B_HEADS // B_KV_HEADS
    qoff = (np.arange(g * Q_BLOCK) % Q_BLOCK)[:, None]
    koff = np.arange(Q_BLOCK + 2 * WINDOW)[None, :]
    return np.stack([np.where(np.abs(koff - d - qoff) <= WINDOW, 0.0, -np.inf) for d in (0, WINDOW, 2 * WINDOW)]
                    ).astype(np.float32)


def _attn_b_kernel(sink_ref, mask_ref, q_ref, k_ref, v_ref, o_ref, *, seq):
    j = pl.program_id(1)
    t = pl.program_id(2)
    qb = Q_BLOCK
    g = B_HEADS // B_KV_HEADS
    band = qb + 2 * WINDOW
    nsub = q_ref.shape[0] // qb
    hrow = lax.broadcasted_iota(jnp.int32, (g * qb, 1), 0) // qb
    sink = jnp.zeros((g * qb, 1), _F32)
    for h in range(g):
        sink = jnp.where(hrow == h, sink_ref[j * g + h] * _LOG2E, sink)
    for u in range(nsub):
        blk = t * nsub + u
        q0 = blk * qb
        start = pl.multiple_of(jnp.clip(q0 - WINDOW, 0, seq - band), LANES)
        kind = jnp.where(blk == 0, 0, jnp.where(blk == seq // qb - 1, 2, 1))
        kb = k_ref[pl.ds(start, band), :]
        vb = v_ref[pl.ds(start, band), :]
        qs = _stack_heads(q_ref[u * qb:(u + 1) * qb, :])
        s = _dot_nt(qs, kb) + mask_ref[kind]
        m = jnp.maximum(jnp.max(s, axis=-1, keepdims=True), sink)
        p = jnp.exp2(s - m)
        l = jnp.sum(p, axis=-1, keepdims=True) + jnp.exp2(sink - m)
        o = _dot(p.astype(_BF16), vb) / l
        o_ref[u * qb:(u + 1) * qb, :] = _unstack_heads(o, qb).astype(o_ref.dtype)


def _attn_b(qkv, sink, *, batch, seq, tq):
    M = qkv.shape[0]
    nt = seq // tq
    assert tq % Q_BLOCK == 0 and seq // Q_BLOCK >= 2
    masks = jnp.asarray(_b_band_masks())
    gw = (B_HEADS // B_KV_HEADS) * HEAD_DIM
    base = (A_HEADS + 4 * A_KV_HEADS) * HEAD_DIM
    q_col = base // gw
    k_col = (base + B_HEADS * HEAD_DIM) // LANES
    v_col = k_col + B_KV_HEADS
    return pl.pallas_call(
        functools.partial(_attn_b_kernel, seq=seq),
        grid=(batch, B_KV_HEADS, nt),
        in_specs=[
            pl.BlockSpec(memory_space=pltpu.SMEM),
            pl.BlockSpec(masks.shape, lambda b, j, t: (0, 0, 0), pipeline_mode=pl.Buffered(1)),
            pl.BlockSpec((tq, gw), lambda b, j, t: (b * nt + t, q_col + j)),
            pl.BlockSpec((seq, LANES), lambda b, j, t: (b, k_col + j)),
            pl.BlockSpec((seq, LANES), lambda b, j, t: (b, v_col + j)),
        ],
        out_specs=pl.BlockSpec((tq, gw), lambda b, j, t: (b * nt + t, j)),
        out_shape=jax.ShapeDtypeStruct((M, B_HEADS * HEAD_DIM), _BF16),
        compiler_params=_params("parallel", "parallel", "parallel"),
        name="attn_b",
    )(sink, masks, qkv, qkv, qkv)


C_GROUP = 4
C_BAND = C_GROUP + NA_KH


def _attn_c_kernel(q_ref, k_ref, v_ref, bias_ref, o_ref, *, rows):
    nq = C_GROUP * GRID_W
    nk = C_BAND * GRID_W
    groups = rows // C_GROUP

    def band(g):
        k0 = min(max(g * C_GROUP - NA_KH // 2, 0), rows - C_BAND) * GRID_W
        return slice(k0, k0 + nk)

    for g in range(groups):
        qs = _stack_heads(q_ref[g * nq:(g + 1) * nq, :])
        s = _dot_nt(qs, k_ref[band(g), :])
        kind = 0 if g == 0 else (2 if g == groups - 1 else 1)
        o = _softmax2_pv(s + bias_ref[kind], v_ref[band(g), :])
        o_ref[g * nq:(g + 1) * nq, :] = _unstack_heads(o, nq).astype(o_ref.dtype)


def _attn_c(qkv, bias, *, batch, seq):
    M = qkv.shape[0]
    width = qkv.shape[1] // 3
    npair = width // LANES
    rows = seq // GRID_W
    return pl.pallas_call(
        functools.partial(_attn_c_kernel, rows=rows),
        grid=(batch, npair),
        in_specs=[
            pl.BlockSpec((seq, LANES), lambda b, p: (b, p)),
            pl.BlockSpec((seq, LANES), lambda b, p: (b, npair + p)),
            pl.BlockSpec((seq, LANES), lambda b, p: (b, 2 * npair + p)),
            pl.BlockSpec((None,) + bias.shape[1:], lambda b, p: (p, 0, 0, 0)),
        ],
        out_specs=pl.BlockSpec((seq, LANES), lambda b, p: (b, p)),
        out_shape=jax.ShapeDtypeStruct((M, width), _BF16),
        compiler_params=_params("parallel", "parallel"),
        name="attn_c",
    )(qkv, qkv, qkv, bias)


def _c_window(kind, u):
    if kind == 0:
        return 0, NA_KH, 0
    if kind == 1:
        return u, u + NA_KH, -(NA_KH // 2)
    return C_GROUP, C_GROUP + NA_KH, C_GROUP - C_BAND


def _c_bias_kernel(rpb_ref, o_ref):
    nq = C_GROUP * GRID_W
    w = lax.broadcasted_iota(jnp.int32, (GRID_W, LANES), 0)
    lane = lax.broadcasted_iota(jnp.int32, (GRID_W, LANES), 1)
    kc = jnp.bitwise_and(lane, GRID_W - 1)
    upper = lane >= GRID_W
    cs = jnp.clip(w - NA_KW // 2, 0, GRID_W - NA_KW)
    inside = (kc >= cs) & (kc < cs + NA_KW)
    for hh in range(2):
        lo_t, hi_t = [], []
        for i in range(2 * NA_KH - 1):
            r = jnp.broadcast_to(rpb_ref[hh, i:i + 1, :], (GRID_W, LANES))
            lo_t.append(pltpu.roll(r, LANES - (NA_KW - 1), 1, stride=1, stride_axis=0))
            hi_t.append(pltpu.roll(r, GRID_W - (NA_KW - 1), 1, stride=1, stride_axis=0))
        neg = jnp.full((GRID_W, LANES), -jnp.inf, _F32)
        for kind in range(3):
            for u in range(C_GROUP):
                lo, hi, shift = _c_window(kind, u)
                for b2 in range(C_BAND // 2):
                    halves = []
                    for b, table in ((2 * b2, lo_t), (2 * b2 + 1, hi_t)):
                        halves.append(table[shift + b - u + NA_KH - 1] if lo <= b < hi else neg)
                    tile = jnp.where(inside, jnp.where(upper, halves[1], halves[0]) * _LOG2E, neg)
                    r0 = hh * nq + u * GRID_W
                    o_ref[kind, r0:r0 + GRID_W, b2 * LANES:(b2 + 1) * LANES] = tile


def _c_bias_table(rpb):
    H, nr, nc = rpb.shape
    assert nr == 2 * NA_KH - 1 and nc == 2 * NA_KW - 1 and 2 * GRID_W == LANES and H % 2 == 0 and C_BAND % 2 == 0
    rpb_pad = jnp.pad(rpb, ((0, 0), (0, 0), (0, LANES - nc)))
    shape = (3, 2 * C_GROUP * GRID_W, C_BAND * GRID_W)
    return pl.pallas_call(
        _c_bias_kernel,
        grid=(H // 2,),
        in_specs=[pl.BlockSpec((2, nr, LANES), lambda p: (p, 0, 0))],
        out_specs=pl.BlockSpec((None,) + shape, lambda p: (p, 0, 0, 0)),
        out_shape=jax.ShapeDtypeStruct((H // 2,) + shape, _F32),
        compiler_params=_params("parallel"),
        name="c_bias",
    )(rpb_pad)


def _rope_tables(ang):
    half = HEAD_DIM // 2
    cos, sin = jnp.cos(ang), jnp.sin(ang)
    zero = jnp.zeros_like(sin)
    cos_h = jnp.concatenate([cos, cos], axis=-1)
    sin_lo = jnp.concatenate([-sin, zero], axis=-1)
    sin_hi = jnp.concatenate([zero, sin], axis=-1)
    assert cos_h.shape[-1] == 2 * half
    return tuple(jnp.concatenate([t, t], axis=-1).astype(_F32) for t in (cos_h, sin_lo, sin_hi))


def _rope_angles(pos, dim):
    inv = ROPE_THETA ** (-jnp.arange(0, dim, 2, dtype=_F32) / dim)
    return pos[:, None] * inv[None, :]


def _ab_columns():
    d = HEAD_DIM
    cols, c = [], 0
    for n_q, n_kv in ((A_HEADS, A_KV_HEADS), (B_HEADS, B_KV_HEADS)):
        cols += list(range(c, c + n_q * d))
        c += n_q * d
        for _ in range(2):
            for h in range(n_kv):
                cols += 2 * list(range(c + h * d, c + (h + 1) * d))
            c += n_kv * d
    return np.asarray(cols, dtype=np.int32)


def _pick(n, pref):
    t = min(n, pref)
    assert n % t == 0, (n, pref)
    return t


def _tiles(batch, seq, d_ff, mem_tokens):
    m = batch * seq
    return dict(
        ffn_rows=_pick(m, 512),
        ffn_cols=_pick(d_ff, 2 * LANES),
        proj_rows=_pick(m, 1024),
        proj_cols=2 * 2 * LANES,
        seq_rows=_pick(seq, 1024),
        mem_rows=_pick(batch * mem_tokens, 512),
        a_queries=_pick(seq, 512), a_keys_bounded=_pick(seq, 256), a_keys_running_max=_pick(seq, 512),
        b_queries=_pick(seq, 8 * Q_BLOCK),
    )


def kernel(x, mem, ln_g, ln_b, ffn_w_gate, ffn_w_up, ffn_w_down, ab_w_in, ab_w_out, ab_q_gain, ab_k_gain, ab_sink,
           c_w_in, c_w_out, c_rpb, mem_w_q, mem_w_kv, mem_w_o):
    B, S, D = x.shape
    depth = ln_g.shape[0]
    M = B * S
    MT = mem.shape[1]
    rows = S // GRID_W
    assert S % GRID_W == 0 and rows >= C_BAND and rows % C_GROUP == 0 and NA_KH // 2 == C_GROUP
    assert S % Q_BLOCK == 0 and S >= Q_BLOCK + 2 * WINDOW
    assert D % (MEM_HEADS * LANES) == 0 and D == 2 * A_HEADS * HEAD_DIM
    alpha = (2.0 * depth) ** 0.25

    t = jnp.arange(S)
    row = (t // GRID_W).astype(_F32)
    colp = (t % GRID_W).astype(_F32)
    ang_2d = jnp.concatenate([_rope_angles(row, HEAD_DIM // 2), _rope_angles(colp, HEAD_DIM // 2)], axis=-1)
    ang_1d = _rope_angles(t.astype(_F32), HEAD_DIM)
    tabs2 = _rope_tables(ang_2d)
    tabs1 = _rope_tables(ang_1d)
    lane_head = np.arange(LANES) // HEAD_DIM
    bd = jnp.asarray((lane_head[:, None] == lane_head[None, :]) / HEAD_DIM, dtype=_BF16)

    wg = ffn_w_gate.astype(_BF16)
    wu = ffn_w_up.astype(_BF16)
    wd = ffn_w_down.astype(_BF16)
    ab_cols = _ab_columns()

    tiles = _tiles(B, S, wg.shape[-1], MT)
    ffn = functools.partial(_ffn_ln, alpha=alpha, tm=tiles["ffn_rows"], tf=tiles["ffn_cols"])
    proj = functools.partial(_proj, tn=tiles["proj_cols"])

    xf = x.reshape(M, D)
    memf = mem.reshape(B * MT, D)
    for i in range(depth):
        j = i // 2
        g = lambda k: ln_g[i, k].reshape(1, D)
        bb = lambda k: ln_b[i, k].reshape(1, D)
        xf = ffn(xf, wg, wu, wd, g(0), bb(0), i, 0)
        if i % 2 == 0:
            w_ext = ab_w_in[j][:, ab_cols].astype(_BF16)
            qg = jnp.tile(ab_q_gain[j], LANES // HEAD_DIM).reshape(1, LANES)
            kg = jnp.tile(ab_k_gain[j], LANES // HEAD_DIM).reshape(1, LANES)
            qkv = _ab_proj(xf, w_ext, bd, qg, kg, tabs2, tabs1, seq=S, tm=tiles["seq_rows"])
            attn_a = functools.partial(_attn_a, batch=B, seq=S, tq=tiles["a_queries"])
            out_a = lax.cond(_a_score_bound(ab_q_gain[j], ab_k_gain[j]) <= A_SCORE_BOUND,
                             functools.partial(attn_a, kc=tiles["a_keys_bounded"], bounded=True),
                             functools.partial(attn_a, kc=tiles["a_keys_running_max"], bounded=False), qkv)
            out_b = _attn_b(qkv, ab_sink[j], batch=B, seq=S, tq=tiles["b_queries"])
            w_out = ab_w_out[j].astype(_BF16)
            na = A_HEADS * HEAD_DIM
            acts, ws = [out_a, out_b], [w_out[:na], w_out[na:]]
        else:
            width = c_w_in.shape[-1] // 3
            qkv = proj(xf, c_w_in[j].astype(_BF16), tm=tiles["proj_rows"], scaled=width,
                       scale=HEAD_DIM ** -0.5 * _LOG2E)
            acts, ws = [_attn_c(qkv, _c_bias_table(c_rpb[j]), batch=B, seq=S)], [c_w_out[j].astype(_BF16)]
        kv = proj(memf, mem_w_kv[i].astype(_BF16), tm=tiles["mem_rows"])
        xf = _mix_mem(acts, ws, xf, g(1), bb(1), mem_w_q[i].astype(_BF16), kv, mem_w_o[i].astype(_BF16), g(2), bb(2),
                      alpha=alpha, batch=B, seq=S, tm=tiles["seq_rows"])
        xf = ffn(xf, wg, wu, wd, g(3), bb(3), i, 1)
    return xf.reshape(B, S, D)
```

```python
import functools

import numpy as np
import jax
import jax.numpy as jnp
from jax import lax
from jax.experimental import pallas as pl
from jax.experimental.pallas import tpu as pltpu

GRID_W = 64
HEAD_DIM = 64
A_HEADS = 8
A_KV_HEADS = 2
B_HEADS = 8
B_KV_HEADS = 2
Q_BLOCK = 128
WINDOW = 128
NA_KH = 8
NA_KW = 16
MEM_HEADS = 4
ROPE_THETA = 10000.0
LN_EPS = 1e-5
RMS_EPS = 1e-6

LANES = 128
VMEM_LIMIT_BYTES = 48 * 1024 * 1024

_BF16 = jnp.bfloat16
_F32 = jnp.float32
_NT = (((1,), (1,)), ((), ()))
_LOG2E = 1.4426950408889634


def _dot(a, b):
    return jnp.dot(a, b, preferred_element_type=_F32)


def _dot_nt(a, b):
    return lax.dot_general(a, b, _NT, preferred_element_type=_F32)


def _params(*sem):
    return pltpu.CompilerParams(dimension_semantics=sem, vmem_limit_bytes=VMEM_LIMIT_BYTES)


def _layer_norm(z, g, b):
    mu = jnp.mean(z, axis=-1, keepdims=True)
    zc = z - mu
    var = jnp.mean(zc * zc, axis=-1, keepdims=True)
    return zc * lax.rsqrt(var + LN_EPS) * g + b


_FFN_OUT_SLABS = 2


def _ffn_ln_kernel(x_ref, wg_ref, wu_ref, wd_ref, g_ref, b_ref, o_ref, act_ref, *, alpha, tf):
    xb = x_ref[...].astype(_BF16)
    for c in range(0, wg_ref.shape[1], tf):
        gate = _dot(xb, wg_ref[:, c:c + tf])
        up = _dot(xb, wu_ref[:, c:c + tf])
        act_ref[:, c:c + tf] = (gate * jax.nn.sigmoid(gate) * up).astype(_BF16)
    slab = x_ref.shape[0] // _FFN_OUT_SLABS
    for r in range(0, x_ref.shape[0], slab):
        y = _dot(act_ref[r:r + slab, :], wd_ref[...])
        o_ref[r:r + slab, :] = _layer_norm(alpha * x_ref[r:r + slab, :] + 0.5 * y, g_ref[...], b_ref[...])


def _ffn_ln(x, wg, wu, wd, g, b, layer, half, *, alpha, tm, tf):
    M, D = x.shape
    F = wg.shape[-1]
    resident = pl.Buffered(1)
    return pl.pallas_call(
        functools.partial(_ffn_ln_kernel, alpha=alpha, tf=tf),
        grid=(M // tm,),
        in_specs=[
            pl.BlockSpec((tm, D), lambda i: (i, 0)),
            pl.BlockSpec((None, None, D, F), lambda i: (layer, half, 0, 0), pipeline_mode=resident),
            pl.BlockSpec((None, None, D, F), lambda i: (layer, half, 0, 0), pipeline_mode=resident),
            pl.BlockSpec((None, None, F, D), lambda i: (layer, half, 0, 0), pipeline_mode=resident),
            pl.BlockSpec((1, D), lambda i: (0, 0)),
            pl.BlockSpec((1, D), lambda i: (0, 0)),
        ],
        out_specs=pl.BlockSpec((tm, D), lambda i: (i, 0)),
        out_shape=jax.ShapeDtypeStruct((M, D), _F32),
        scratch_shapes=[pltpu.VMEM((tm, F), _BF16)],
        compiler_params=_params("parallel"),
        name="ffn_ln",
    )(x, wg, wu, wd, g, b)


def _proj_kernel(x_ref, w_ref, o_ref, *, tn, scaled, scale):
    xb = x_ref[...].astype(_BF16)
    for c in range(0, w_ref.shape[1], tn):
        y = _dot(xb, w_ref[:, c:c + tn])
        if c < scaled:
            y = y * scale
        o_ref[:, c:c + tn] = y.astype(o_ref.dtype)


def _proj(x, w, *, tm, tn, scaled=0, scale=1.0):
    M, K = x.shape
    N = w.shape[1]
    assert scaled % tn == 0
    return pl.pallas_call(
        functools.partial(_proj_kernel, tn=tn, scaled=scaled, scale=scale),
        grid=(M // tm,),
        in_specs=[pl.BlockSpec((tm, K), lambda i: (i, 0)),
                  pl.BlockSpec((K, N), lambda i: (0, 0), pipeline_mode=pl.Buffered(1))],
        out_specs=pl.BlockSpec((tm, N), lambda i: (i, 0)),
        out_shape=jax.ShapeDtypeStruct((M, N), _BF16),
        compiler_params=_params("parallel"),
        name="proj",
    )(x, w)


def _softmax2_pv(s, v):
    m = jnp.max(s, axis=-1, keepdims=True)
    p = jnp.exp2(s - m)
    l = jnp.sum(p, axis=-1, keepdims=True)
    return _dot(p.astype(_BF16), v) / l


_MIX_SLABS = 4


def _mix_mem_kernel(*refs, alpha, n_in, scale):
    a_refs = refs[:n_in]
    w_refs = refs[n_in:2 * n_in]
    x_ref, g1_ref, b1_ref, wq_ref, k_ref, v_ref, wo_ref, g2_ref, b2_ref, o_ref = refs[2 * n_in:]
    hd = x_ref.shape[1] // MEM_HEADS
    step = x_ref.shape[0] // _MIX_SLABS
    slabs = [slice(r, r + step) for r in range(0, x_ref.shape[0], step)]
    ys = []
    for rows in slabs:
        y = _dot(a_refs[0][rows, :], w_refs[0][...])
        for a_ref, w_ref in zip(a_refs[1:], w_refs[1:]):
            y = y + _dot(a_ref[rows, :], w_ref[...])
        ys.append(y)
    x1s = [_layer_norm(alpha * x_ref[rows, :] + y, g1_ref[...], b1_ref[...]) for rows, y in zip(slabs, ys)]
    qs = [(_dot(x1.astype(_BF16), wq_ref[...]) * scale).astype(_BF16) for x1 in x1s]
    os = []
    for q in qs:
        heads = []
        for h in range(MEM_HEADS):
            sl = slice(h * hd, (h + 1) * hd)
            heads.append(_softmax2_pv(_dot_nt(q[:, sl], k_ref[:, sl]), v_ref[:, sl]).astype(_BF16))
        os.append(jnp.concatenate(heads, axis=1))
    y2s = [_dot(o, wo_ref[...]) for o in os]
    for rows, x1, y2 in zip(slabs, x1s, y2s):
        o_ref[rows, :] = _layer_norm(alpha * x1 + y2, g2_ref[...], b2_ref[...])


def _mix_mem(acts, ws, x, g1, b1, wq, kv, wo, g2, b2, *, alpha, batch, seq, tm, kv_col=0):
    M, D = x.shape
    n_in = len(acts)
    mt = kv.shape[0] // batch
    nt = seq // tm
    row = lambda b, t: (b * nt + t, 0)
    const = lambda b, t: (0, 0)
    resident = pl.Buffered(1)
    in_specs = [pl.BlockSpec((tm, a.shape[1]), row) for a in acts]
    in_specs += [pl.BlockSpec(w.shape, const, pipeline_mode=resident) for w in ws]
    in_specs += [
        pl.BlockSpec((tm, D), row), pl.BlockSpec((1, D), const), pl.BlockSpec((1, D), const),
        pl.BlockSpec((D, D), const, pipeline_mode=resident),
        pl.BlockSpec((mt, D), lambda b, t: (b, kv_col)),
        pl.BlockSpec((mt, D), lambda b, t: (b, kv_col + 1)),
        pl.BlockSpec((D, D), const, pipeline_mode=resident),
        pl.BlockSpec((1, D), const), pl.BlockSpec((1, D), const),
    ]
    return pl.pallas_call(
        functools.partial(_mix_mem_kernel, alpha=alpha, n_in=n_in, scale=(D // MEM_HEADS) ** -0.5 * _LOG2E),
        grid=(batch, nt),
        in_specs=in_specs,
        out_specs=pl.BlockSpec((tm, D), row),
        out_shape=jax.ShapeDtypeStruct((M, D), _F32),
        compiler_params=_params("parallel", "parallel"),
        name="mix_mem",
    )(*acts, *ws, x, g1, b1, wq, kv, kv, wo, g2, b2)


_AB_CHUNK = 256
_AB_LAYOUT = (("qa", 512), ("ka", 256), ("va", 256), ("qb", 512), ("kb", 256), ("vb", 256))


def _rope(y, cos, sin_lo, sin_hi):
    return y * cos + pltpu.roll(y, 96, axis=1) * sin_lo + pltpu.roll(y, 32, axis=1) * sin_hi


def _head_mean_square(y, bd):
    sq = y * y
    hi = sq.astype(_BF16)
    lo = (sq - hi.astype(_F32)).astype(_BF16)
    return _dot(hi, bd) + _dot(lo, bd)


_AB_SLABS = 2


def _ab_proj_kernel(x_ref, w_ref, bd_ref, qg_ref, kg_ref, c2_ref, sl2_ref, sh2_ref, c1_ref, sl1_ref, sh1_ref,
                    o_ref, *, scale):
    bd = bd_ref[...]
    chunks, col = [], 0
    for name, width in _AB_LAYOUT:
        chunks += [(name, c) for c in range(col, col + width, _AB_CHUNK)]
        col += width
    normed = [nc for nc in chunks if nc[0] in ("qa", "ka")]
    plain = [nc for nc in chunks if nc[0] not in ("qa", "ka")]
    groups = range(0, _AB_CHUNK, LANES)
    step = x_ref.shape[0] // _AB_SLABS
    slabs = [slice(r, r + step) for r in range(0, x_ref.shape[0], step)]
    xbs = [x_ref[rows, :].astype(_BF16) for rows in slabs]

    def finish(rows, name, c, h, y):
        if name in ("qa", "ka"):
            y = _rope(y, c2_ref[rows, :], sl2_ref[rows, :], sh2_ref[rows, :])
        elif name in ("qb", "kb"):
            y = _rope(y, c1_ref[rows, :], sl1_ref[rows, :], sh1_ref[rows, :])
        if name in ("qa", "qb"):
            y = y * (scale * _LOG2E)
        o_ref[rows, c + h:c + h + LANES] = y.astype(o_ref.dtype)

    raw = [{c: _dot(xb, w_ref[:, c:c + _AB_CHUNK]) for _, c in normed} for xb in xbs]
    ms = [{(c, h): _head_mean_square(r[c][:, h:h + LANES], bd) for _, c in normed for h in groups} for r in raw]
    for name, c in plain:
        for rows, xb in zip(slabs, xbs):
            y2 = _dot(xb, w_ref[:, c:c + _AB_CHUNK])
            for h in groups:
                finish(rows, name, c, h, y2[:, h:h + LANES])
    for name, c in normed:
        gain = qg_ref[...] if name == "qa" else kg_ref[...]
        for rows, r, m in zip(slabs, raw, ms):
            for h in groups:
                finish(rows, name, c, h, r[c][:, h:h + LANES] * lax.rsqrt(m[(c, h)] + RMS_EPS) * gain)


def _ab_proj(x, w_ext, bd, qg, kg, tabs2, tabs1, *, seq, tm):
    M, D = x.shape
    N = w_ext.shape[1]
    nt = seq // tm
    tab_spec = pl.BlockSpec((tm, LANES), lambda i: (i % nt, 0))
    const = lambda shape: pl.BlockSpec(shape, lambda i: (0, 0))
    return pl.pallas_call(
        functools.partial(_ab_proj_kernel, scale=HEAD_DIM ** -0.5),
        grid=(M // tm,),
        in_specs=[pl.BlockSpec((tm, D), lambda i: (i, 0)), const((D, N)), const((LANES, LANES)),
                  const((1, LANES)), const((1, LANES))] + [tab_spec] * 6,
        out_specs=pl.BlockSpec((tm, N), lambda i: (i, 0)),
        out_shape=jax.ShapeDtypeStruct((M, N), _BF16),
        compiler_params=_params("parallel"),
        name="ab_proj",
    )(x, w_ext, bd, qg, kg, *tabs2, *tabs1)


def _stack_heads(q):
    t = q.shape[0]
    lo = lax.broadcasted_iota(jnp.int32, (t, LANES), 1) < HEAD_DIM
    zero = jnp.zeros((t, LANES), q.dtype)
    parts = []
    for c in range(0, q.shape[1], LANES):
        pair = q[:, c:c + LANES]
        parts += [jnp.where(lo, pair, zero), jnp.where(lo, zero, pair)]
    return jnp.concatenate(parts, axis=0)


def _unstack_heads(o, t):
    lo = lax.broadcasted_iota(jnp.int32, (t, LANES), 1) < HEAD_DIM
    n = o.shape[0] // t
    return jnp.concatenate(
        [jnp.where(lo, o[(2 * i) * t:(2 * i + 1) * t], o[(2 * i + 1) * t:(2 * i + 2) * t]) for i in range(n // 2)],
        axis=1)


def _attn_a_kernel(q_ref, k_ref, v_ref, o_ref, *, kc):
    tq = q_ref.shape[0]
    qs = _stack_heads(q_ref[...])
    m = l = acc = None
    for c in range(0, k_ref.shape[0], kc):
        s = _dot_nt(qs, k_ref[c:c + kc, :])
        mc = jnp.max(s, axis=-1, keepdims=True)
        m_new = mc if m is None else jnp.maximum(m, mc)
        p = jnp.exp2(s - m_new)
        lc = p[:, :LANES]
        for i in range(LANES, kc, LANES):
            lc = lc + p[:, i:i + LANES]
        pv = _dot(p.astype(_BF16), v_ref[c:c + kc, :])
        if m is None:
            l, acc = lc, pv
        else:
            corr = jnp.exp2(m - m_new)
            l, acc = l * corr + lc, acc * corr + pv
        m = m_new
    o_ref[...] = _unstack_heads(acc / jnp.sum(l, axis=-1, keepdims=True), tq).astype(o_ref.dtype)


_A_SLABS = 2


def _attn_a_bounded_kernel(q_ref, k_ref, v_ref, o_ref, *, kc):
    ts = q_ref.shape[0] // _A_SLABS
    slabs = [slice(r, r + ts) for r in range(0, q_ref.shape[0], ts)]
    qss = [_stack_heads(q_ref[rows, :]) for rows in slabs]
    ls = [None] * len(slabs)
    accs = [None] * len(slabs)
    for c in range(0, k_ref.shape[0], kc):
        for i, qs in enumerate(qss):
            p = jnp.exp2(_dot_nt(qs, k_ref[c:c + kc, :]))
            lc = p[:, :LANES]
            for j in range(LANES, kc, LANES):
                lc = lc + p[:, j:j + LANES]
            pv = _dot(p.astype(_BF16), v_ref[c:c + kc, :])
            ls[i], accs[i] = (lc, pv) if ls[i] is None else (ls[i] + lc, accs[i] + pv)
    for rows, l, acc in zip(slabs, ls, accs):
        o_ref[rows, :] = _unstack_heads(acc / jnp.sum(l, axis=-1, keepdims=True), ts).astype(o_ref.dtype)


A_SCORE_BOUND = 60.0


def _a_score_bound(q_gain, k_gain):
    return 1.02 * HEAD_DIM ** 0.5 * _LOG2E * jnp.max(jnp.abs(q_gain)) * jnp.max(jnp.abs(k_gain))


def _attn_a(qkv, *, batch, seq, tq, kc, bounded):
    M = qkv.shape[0]
    nt = seq // tq
    gw = (A_HEADS // A_KV_HEADS) * HEAD_DIM
    k_col = A_HEADS * HEAD_DIM // LANES
    v_col = k_col + A_KV_HEADS
    return pl.pallas_call(
        functools.partial(_attn_a_bounded_kernel if bounded else _attn_a_kernel, kc=kc),
        grid=(batch, A_KV_HEADS, nt),
        in_specs=[
            pl.BlockSpec((tq, gw), lambda b, j, t: (b * nt + t, j)),
            pl.BlockSpec((seq, LANES), lambda b, j, t: (b, k_col + j)),
            pl.BlockSpec((seq, LANES), lambda b, j, t: (b, v_col + j)),
        ],
        out_specs=pl.BlockSpec((tq, gw), lambda b, j, t: (b * nt + t, j)),
        out_shape=jax.ShapeDtypeStruct((M, A_HEADS * HEAD_DIM), _BF16),
        compiler_params=_params("parallel", "parallel", "parallel"),
        name="attn_a",
    )(qkv, qkv, qkv)


def _b_band_masks():
    g = B_HEADS // B_KV_HEADS
    qoff = (np.arange(g * Q_BLOCK) % Q_BLOCK)[:, None]
    koff = np.arange(Q_BLOCK + 2 * WINDOW)[None, :]
    return np.stack([np.where(np.abs(koff - d - qoff) <= WINDOW, 0.0, -np.inf) for d in (0, WINDOW, 2 * WINDOW)]
                    ).astype(np.float32)


def _attn_b_kernel(sink_ref, mask_ref, q_ref, k_ref, v_ref, o_ref, *, seq):
    j = pl.program_id(1)
    t = pl.program_id(2)
    qb = Q_BLOCK
    g = B_HEADS // B_KV_HEADS
    band = qb + 2 * WINDOW
    nsub = q_ref.shape[0] // qb
    hrow = lax.broadcasted_iota(jnp.int32, (g * qb, 1), 0) // qb
    sink = jnp.zeros((g * qb, 1), _F32)
    for h in range(g):
        sink = jnp.where(hrow == h, sink_ref[j * g + h] * _LOG2E, sink)
    for u in range(nsub):
        blk = t * nsub + u
        q0 = blk * qb
        start = pl.multiple_of(jnp.clip(q0 - WINDOW, 0, seq - band), LANES)
        kind = jnp.where(blk == 0, 0, jnp.where(blk == seq // qb - 1, 2, 1))
        kb = k_ref[pl.ds(start, band), :]
        vb = v_ref[pl.ds(start, band), :]
        qs = _stack_heads(q_ref[u * qb:(u + 1) * qb, :])
        s = _dot_nt(qs, kb) + mask_ref[kind]
        m = jnp.maximum(jnp.max(s, axis=-1, keepdims=True), sink)
        p = jnp.exp2(s - m)
        l = jnp.sum(p, axis=-1, keepdims=True) + jnp.exp2(sink - m)
        o = _dot(p.astype(_BF16), vb) / l
        o_ref[u * qb:(u + 1) * qb, :] = _unstack_heads(o, qb).astype(o_ref.dtype)


def _attn_b(qkv, sink, *, batch, seq, tq):
    M = qkv.shape[0]
    nt = seq // tq
    assert tq % Q_BLOCK == 0 and seq // Q_BLOCK >= 2
    masks = jnp.asarray(_b_band_masks())
    gw = (B_HEADS // B_KV_HEADS) * HEAD_DIM
    base = (A_HEADS + 4 * A_KV_HEADS) * HEAD_DIM
    q_col = base // gw
    k_col = (base + B_HEADS * HEAD_DIM) // LANES
    v_col = k_col + B_KV_HEADS
    return pl.pallas_call(
        functools.partial(_attn_b_kernel, seq=seq),
        grid=(batch, B_KV_HEADS, nt),
        in_specs=[
            pl.BlockSpec(memory_space=pltpu.SMEM),
            pl.BlockSpec(masks.shape, lambda b, j, t: (0, 0, 0), pipeline_mode=pl.Buffered(1)),
            pl.BlockSpec((tq, gw), lambda b, j, t: (b * nt + t, q_col + j)),
            pl.BlockSpec((seq, LANES), lambda b, j, t: (b, k_col + j)),
            pl.BlockSpec((seq, LANES), lambda b, j, t: (b, v_col + j)),
        ],
        out_specs=pl.BlockSpec((tq, gw), lambda b, j, t: (b * nt + t, j)),
        out_shape=jax.ShapeDtypeStruct((M, B_HEADS * HEAD_DIM), _BF16),
        compiler_params=_params("parallel", "parallel", "parallel"),
        name="attn_b",
    )(sink, masks, qkv, qkv, qkv)


C_GROUP = 4
C_BAND = C_GROUP + NA_KH


def _attn_c_kernel(q_ref, k_ref, v_ref, bias_ref, o_ref, *, rows):
    nq = C_GROUP * GRID_W
    nk = C_BAND * GRID_W
    groups = rows // C_GROUP

    def band(g):
        k0 = min(max(g * C_GROUP - NA_KH // 2, 0), rows - C_BAND) * GRID_W
        return slice(k0, k0 + nk)

    for g in range(groups):
        qs = _stack_heads(q_ref[g * nq:(g + 1) * nq, :])
        s = _dot_nt(qs, k_ref[band(g), :])
        kind = 0 if g == 0 else (2 if g == groups - 1 else 1)
        o = _softmax2_pv(s + bias_ref[kind], v_ref[band(g), :])
        o_ref[g * nq:(g + 1) * nq, :] = _unstack_heads(o, nq).astype(o_ref.dtype)


def _attn_c(qkv, bias, *, batch, seq):
    M = qkv.shape[0]
    width = qkv.shape[1] // 3
    npair = width // LANES
    rows = seq // GRID_W
    return pl.pallas_call(
        functools.partial(_attn_c_kernel, rows=rows),
        grid=(batch, npair),
        in_specs=[
            pl.BlockSpec((seq, LANES), lambda b, p: (b, p)),
            pl.BlockSpec((seq, LANES), lambda b, p: (b, npair + p)),
            pl.BlockSpec((seq, LANES), lambda b, p: (b, 2 * npair + p)),
            pl.BlockSpec((None,) + bias.shape[1:], lambda b, p: (p, 0, 0, 0)),
        ],
        out_specs=pl.BlockSpec((seq, LANES), lambda b, p: (b, p)),
        out_shape=jax.ShapeDtypeStruct((M, width), _BF16),
        compiler_params=_params("parallel", "parallel"),
        name="attn_c",
    )(qkv, qkv, qkv, bias)


def _c_window(kind, u):
    if kind == 0:
        return 0, NA_KH, 0
    if kind == 1:
        return u, u + NA_KH, -(NA_KH // 2)
    return C_GROUP, C_GROUP + NA_KH, C_GROUP - C_BAND


def _c_bias_kernel(rpb_ref, o_ref):
    nq = C_GROUP * GRID_W
    w = lax.broadcasted_iota(jnp.int32, (GRID_W, LANES), 0)
    lane = lax.broadcasted_iota(jnp.int32, (GRID_W, LANES), 1)
    kc = jnp.bitwise_and(lane, GRID_W - 1)
    upper = lane >= GRID_W
    cs = jnp.clip(w - NA_KW // 2, 0, GRID_W - NA_KW)
    inside = (kc >= cs) & (kc < cs + NA_KW)
    for hh in range(2):
        lo_t, hi_t = [], []
        for i in range(2 * NA_KH - 1):
            r = jnp.broadcast_to(rpb_ref[hh, i:i + 1, :], (GRID_W, LANES))
            lo_t.append(pltpu.roll(r, LANES - (NA_KW - 1), 1, stride=1, stride_axis=0))
            hi_t.append(pltpu.roll(r, GRID_W - (NA_KW - 1), 1, stride=1, stride_axis=0))
        neg = jnp.full((GRID_W, LANES), -jnp.inf, _F32)
        for kind in range(3):
            for u in range(C_GROUP):
                lo, hi, shift = _c_window(kind, u)
                for b2 in range(C_BAND // 2):
                    halves = []
                    for b, table in ((2 * b2, lo_t), (2 * b2 + 1, hi_t)):
                        halves.append(table[shift + b - u + NA_KH - 1] if lo <= b < hi else neg)
                    tile = jnp.where(inside, jnp.where(upper, halves[1], halves[0]) * _LOG2E, neg)
                    r0 = hh * nq + u * GRID_W
                    o_ref[kind, r0:r0 + GRID_W, b2 * LANES:(b2 + 1) * LANES] = tile


def _c_bias_table(rpb):
    H, nr, nc = rpb.shape
    assert nr == 2 * NA_KH - 1 and nc == 2 * NA_KW - 1 and 2 * GRID_W == LANES and H % 2 == 0 and C_BAND % 2 == 0
    rpb_pad = jnp.pad(rpb, ((0, 0), (0, 0), (0, LANES - nc)))
    shape = (3, 2 * C_GROUP * GRID_W, C_BAND * GRID_W)
    return pl.pallas_call(
        _c_bias_kernel,
        grid=(H // 2,),
        in_specs=[pl.BlockSpec((2, nr, LANES), lambda p: (p, 0, 0))],
        out_specs=pl.BlockSpec((None,) + shape, lambda p: (p, 0, 0, 0)),
        out_shape=jax.ShapeDtypeStruct((H // 2,) + shape, _F32),
        compiler_params=_params("parallel"),
        name="c_bias",
    )(rpb_pad)


def _rope_tables(ang):
    half = HEAD_DIM // 2
    cos, sin = jnp.cos(ang), jnp.sin(ang)
    zero = jnp.zeros_like(sin)
    cos_h = jnp.concatenate([cos, cos], axis=-1)
    sin_lo = jnp.concatenate([-sin, zero], axis=-1)
    sin_hi = jnp.concatenate([zero, sin], axis=-1)
    assert cos_h.shape[-1] == 2 * half
    return tuple(jnp.concatenate([t, t], axis=-1).astype(_F32) for t in (cos_h, sin_lo, sin_hi))


def _rope_angles(pos, dim):
    inv = ROPE_THETA ** (-jnp.arange(0, dim, 2, dtype=_F32) / dim)
    return pos[:, None] * inv[None, :]


def _ab_columns():
    d = HEAD_DIM
    cols, c = [], 0
    for n_q, n_kv in ((A_HEADS, A_KV_HEADS), (B_HEADS, B_KV_HEADS)):
        cols += list(range(c, c + n_q * d))
        c += n_q * d
        for _ in range(2):
            for h in range(n_kv):
                cols += 2 * list(range(c + h * d, c + (h + 1) * d))
            c += n_kv * d
    return np.asarray(cols, dtype=np.int32)


def _pick(n, pref):
    t = min(n, pref)
    assert n % t == 0, (n, pref)
    return t


def _tiles(batch, seq, d_ff, mem_tokens):
    m = batch * seq
    return dict(
        ffn_rows=_pick(m, 512),
        ffn_cols=_pick(d_ff, 2 * LANES),
        proj_rows=_pick(m, 1024),
        proj_cols=2 * 2 * LANES,
        seq_rows=_pick(seq, 1024),
        mem_rows=_pick(batch * mem_tokens, 512),
        a_queries=_pick(seq, 512), a_keys_bounded=_pick(seq, 256), a_keys_running_max=_pick(seq, 512),
        b_queries=_pick(seq, 8 * Q_BLOCK),
    )


def kernel(x, mem, ln_g, ln_b, ffn_w_gate, ffn_w_up, ffn_w_down, ab_w_in, ab_w_out, ab_q_gain, ab_k_gain, ab_sink,
           c_w_in, c_w_out, c_rpb, mem_w_q, mem_w_kv, mem_w_o):
    B, S, D = x.shape
    depth = ln_g.shape[0]
    M = B * S
    MT = mem.shape[1]
    rows = S // GRID_W
    assert S % GRID_W == 0 and rows >= C_BAND and rows % C_GROUP == 0 and NA_KH // 2 == C_GROUP
    assert S % Q_BLOCK == 0 and S >= Q_BLOCK + 2 * WINDOW
    assert D % (MEM_HEADS * LANES) == 0 and D == 2 * A_HEADS * HEAD_DIM
    alpha = (2.0 * depth) ** 0.25

    t = jnp.arange(S)
    row = (t // GRID_W).astype(_F32)
    colp = (t % GRID_W).astype(_F32)
    ang_2d = jnp.concatenate([_rope_angles(row, HEAD_DIM // 2), _rope_angles(colp, HEAD_DIM // 2)], axis=-1)
    ang_1d = _rope_angles(t.astype(_F32), HEAD_DIM)
    tabs2 = _rope_tables(ang_2d)
    tabs1 = _rope_tables(ang_1d)
    lane_head = np.arange(LANES) // HEAD_DIM
    bd = jnp.asarray((lane_head[:, None] == lane_head[None, :]) / HEAD_DIM, dtype=_BF16)

    wg = ffn_w_gate.astype(_BF16)
    wu = ffn_w_up.astype(_BF16)
    wd = ffn_w_down.astype(_BF16)
    ab_cols = _ab_columns()

    tiles = _tiles(B, S, wg.shape[-1], MT)
    ffn = functools.partial(_ffn_ln, alpha=alpha, tm=tiles["ffn_rows"], tf=tiles["ffn_cols"])
    proj = functools.partial(_proj, tn=tiles["proj_cols"])

    xf = x.reshape(M, D)
    memf = mem.reshape(B * MT, D)
    for i in range(depth):
        j = i // 2
        g = lambda k: ln_g[i, k].reshape(1, D)
        bb = lambda k: ln_b[i, k].reshape(1, D)
        xf = ffn(xf, wg, wu, wd, g(0), bb(0), i, 0)
        if i % 2 == 0:
            w_ext = ab_w_in[j][:, ab_cols].astype(_BF16)
            qg = jnp.tile(ab_q_gain[j], LANES // HEAD_DIM).reshape(1, LANES)
            kg = jnp.tile(ab_k_gain[j], LANES // HEAD_DIM).reshape(1, LANES)
            qkv = _ab_proj(xf, w_ext, bd, qg, kg, tabs2, tabs1, seq=S, tm=tiles["seq_rows"])
            attn_a = functools.partial(_attn_a, batch=B, seq=S, tq=tiles["a_queries"])
            out_a = lax.cond(_a_score_bound(ab_q_gain[j], ab_k_gain[j]) <= A_SCORE_BOUND,
                             functools.partial(attn_a, kc=tiles["a_keys_bounded"], bounded=True),
                             functools.partial(attn_a, kc=tiles["a_keys_running_max"], bounded=False), qkv)
            out_b = _attn_b(qkv, ab_sink[j], batch=B, seq=S, tq=tiles["b_queries"])
            w_out = ab_w_out[j].astype(_BF16)
            na = A_HEADS * HEAD_DIM
            acts, ws = [out_a, out_b], [w_out[:na], w_out[na:]]
        else:
            width = c_w_in.shape[-1] // 3
            qkv = proj(xf, c_w_in[j].astype(_BF16), tm=tiles["proj_rows"], scaled=width,
                       scale=HEAD_DIM ** -0.5 * _LOG2E)
            acts, ws = [_attn_c(qkv, _c_bias_table(c_rpb[j]), batch=B, seq=S)], [c_w_out[j].astype(_BF16)]
        if i == 0:
            w_kv_all = mem_w_kv.transpose(1, 0, 2).reshape(D, depth * 2 * D).astype(_BF16)
            kv = proj(memf, w_kv_all, tm=_pick(B * MT, 256))
        xf = _mix_mem(acts, ws, xf, g(1), bb(1), mem_w_q[i].astype(_BF16), kv, mem_w_o[i].astype(_BF16), g(2), bb(2),
                      alpha=alpha, batch=B, seq=S, tm=tiles["seq_rows"], kv_col=2 * i)
        xf = ffn(xf, wg, wu, wd, g(3), bb(3), i, 1)
    return xf.reshape(B, S, D)
```
